```python
import jax
import jax.numpy as jnp
from jax import lax
import numpy as np

D_MODEL = 1024
BATCH = 2
SEQ = 8192
DEPTH = 4
DEC_BATCH = 32
DEC_SEQ = 4
PAST_LEN = 8192
PAGE_SIZE = 128

N_HEADS = 16
HEAD_DIM = 64
KV_GROUPS = 4
GROUP_SIZE = N_HEADS // KV_GROUPS
CMP_STRIDE = 16
CMP_BLOCK = 2 * CMP_STRIDE
CMP_HIDDEN = 64
SEL_BLOCK = 64
SEL_TOPN = 16
N_LOCAL = 2
WINDOW = 512
Q_BLOCK = 128
Q_DIM = N_HEADS * HEAD_DIM
KV_DIM = 2 * KV_GROUPS * HEAD_DIM
NSA_IN = Q_DIM + 3 * KV_DIM + 3 * N_HEADS
ATTN_SCALE = HEAD_DIM ** -0.5
CHUNK = 128
GMLP_DIM = 2 * D_MODEL
GMLP_GROUPS = 8
GMLP_GW = GMLP_DIM // GMLP_GROUPS
FFN_DIM = 256 * (-(-8 * D_MODEL // (3 * 256)))
N_NSA = (DEPTH + 1) // 2
N_GMLP = DEPTH // 2
EPS = 1e-6
NEG = -1e30
FORCE = 1e4

kernel_name = 'nsa_gmlp_hybrid_step'


def rmsnorm(x, g):
    xf = x.astype(jnp.float32)
    y = xf * lax.rsqrt(jnp.mean(xf * xf, axis=-1, keepdims=True) + EPS)
    return y.astype(x.dtype) * g


def layernorm(x, g):
    xf = x.astype(jnp.float32)
    xc = xf - jnp.mean(xf, axis=-1, keepdims=True)
    y = xc * lax.rsqrt(jnp.mean(xc * xc, axis=-1, keepdims=True) + EPS)
    return y.astype(x.dtype) * g


def swiglu(h, w_in, w_out):
    gu = h @ w_in
    return (jax.nn.silu(gu[..., :FFN_DIM]) * gu[..., FFN_DIM:]) @ w_out


def nsa_project(h, w_in):
    B, T, _ = h.shape
    z = h @ w_in
    q = z[..., :Q_DIM].reshape(B, T, KV_GROUPS, GROUP_SIZE, HEAD_DIM)
    kv = [z[..., Q_DIM + j * KV_DIM:Q_DIM + (j + 1) * KV_DIM].reshape(B, T, 2, KV_GROUPS, HEAD_DIM)
          for j in range(3)]
    gates = jax.nn.sigmoid(z[..., Q_DIM + 3 * KV_DIM:].astype(jnp.float32)).astype(z.dtype)
    gates = gates.reshape(B, T, KV_GROUPS, GROUP_SIZE, 3)
    return q, kv[0], kv[1], kv[2], gates


def compress(rows, w1, w2, pe):
    B, T = rows.shape[:2]
    n = T // CMP_STRIDE
    ch = rows[:, :n * CMP_STRIDE].reshape(B, n, CMP_STRIDE, KV_GROUPS, HEAD_DIM)
    hid = (jnp.einsum('bnlgd,lde->bnge', ch[:, :-1] + pe[:CMP_STRIDE, None], w1[:CMP_STRIDE])
           + jnp.einsum('bnlgd,lde->bnge', ch[:, 1:] + pe[CMP_STRIDE:, None], w1[CMP_STRIDE:]))
    return jnp.einsum('bnge,ed->bngd', jax.nn.gelu(hid), w2)


def overlap_matrix(nc, ns):
    cs = jnp.arange(nc) * CMP_STRIDE
    ss = jnp.arange(ns) * SEL_BLOCK
    ov = jnp.minimum(cs[:, None] + CMP_BLOCK, ss[None, :] + SEL_BLOCK) - jnp.maximum(cs[:, None], ss[None, :])
    return jnp.clip(ov, 0).astype(jnp.float32) / CMP_BLOCK


def cmp_attend(q, qpos, kc, vc):
    nc = kc.shape[1]
    s = jnp.einsum('bqgrd,bngd->bgrqn', q, kc).astype(jnp.float32) * ATTN_SCALE
    end = jnp.arange(nc) * CMP_STRIDE + CMP_BLOCK - 1
    mask = end[None, :] <= qpos[:, None]
    p = jax.nn.softmax(jnp.where(mask, s, NEG), axis=-1)
    p = jnp.where(jnp.any(mask, axis=-1)[:, None], p, 0.0)
    o = jnp.einsum('bgrqn,bngd->bqgrd', p.astype(vc.dtype), vc)
    return o, p


def select_positions(p_cmp, qpos, ov, k_top):
    imp = jnp.einsum('bgrqn,ns->bgqs', p_cmp, ov)
    blk = jnp.arange(ov.shape[1])[None, :]
    cur = (qpos // SEL_BLOCK)[:, None]
    valid = blk <= cur
    forced = (blk == 0) | (blk > cur - N_LOCAL)
    score = jnp.where(valid, jnp.where(forced, FORCE, imp), NEG)
    _, idx = lax.top_k(score, k_top)
    pos = idx[..., None] * SEL_BLOCK + jnp.arange(SEL_BLOCK)
    return pos.reshape(idx.shape[:-1] + (k_top * SEL_BLOCK,))


def gather_rows(kt, pos):
    return jax.vmap(jax.vmap(lambda a, p: a[p]))(kt, pos)


def paged_gather(pool, page_table, new_rows, pos):
    past = page_table.shape[1] * PAGE_SIZE
    db, g = pos.shape[:2]
    b_idx = jnp.arange(db)[:, None, None, None]
    g_idx = jnp.arange(g)[None, :, None, None]
    p_old = jnp.minimum(pos, past - 1)
    phys = page_table[b_idx, p_old // PAGE_SIZE]
    old = pool[phys, p_old % PAGE_SIZE, :, g_idx]
    j = jnp.clip(pos - past, 0, new_rows.shape[1] - 1)
    new = new_rows[b_idx, j, :, g_idx]
    return jnp.where((pos < past)[..., None, None], old, new)


def sel_attend(q, qpos, ks, vs, pos):
    s = jnp.einsum('bqgrd,bgqkd->bgrqk', q, ks).astype(jnp.float32) * ATTN_SCALE
    mask = (pos <= qpos[None, None, :, None])[:, :, None]
    p = jax.nn.softmax(jnp.where(mask, s, NEG), axis=-1)
    return jnp.einsum('bgrqk,bgqkd->bqgrd', p.astype(vs.dtype), vs)


def win_attend(q, qpos, kw, vw, kpos):
    s = jnp.einsum('bqgrd,bsgd->bgrqs', q, kw).astype(jnp.float32) * ATTN_SCALE
    d = qpos[:, None] - kpos[None, :]
    mask = (d >= 0) & (d < WINDOW) & (kpos[None, :] >= 0)
    p = jax.nn.softmax(jnp.where(mask, s, NEG), axis=-1)
    return jnp.einsum('bgrqs,bsgd->bqgrd', p.astype(vw.dtype), vw)


def combine(g, o_c, o_s, o_w):
    B, T = o_c.shape[:2]
    o = g[..., 0:1] * o_c + g[..., 1:2] * o_s + g[..., 2:3] * o_w
    return o.reshape(B, T, Q_DIM)


def nsa_prompt(q, kv_c, kv_s, kv_w, gates, w1, w2, pe):
    B, T = q.shape[:2]
    kc = compress(kv_c[:, :, 0], w1[0], w2[0], pe[0])
    vc = compress(kv_c[:, :, 1], w1[1], w2[1], pe[1])
    ov = overlap_matrix(kc.shape[1], -(-T // SEL_BLOCK))
    k_top = min(SEL_TOPN, ov.shape[1])
    ks = kv_s[:, :, 0].transpose(0, 2, 1, 3)
    vs = kv_s[:, :, 1].transpose(0, 2, 1, 3)
    kw = jnp.pad(kv_w, ((0, 0), (WINDOW, 0), (0, 0), (0, 0), (0, 0)))

    def block(i):
        s0 = i * Q_BLOCK
        qb = lax.dynamic_slice_in_dim(q, s0, Q_BLOCK, 1)
        gb = lax.dynamic_slice_in_dim(gates, s0, Q_BLOCK, 1)
        qpos = s0 + jnp.arange(Q_BLOCK)
        o_c, p_c = cmp_attend(qb, qpos, kc, vc)
        pos = select_positions(p_c, qpos, ov, k_top)
        o_s = sel_attend(qb, qpos, gather_rows(ks, pos), gather_rows(vs, pos), pos)
        wb = lax.dynamic_slice_in_dim(kw, s0, WINDOW + Q_BLOCK, 1)
        kpos = s0 - WINDOW + jnp.arange(WINDOW + Q_BLOCK)
        o_w = win_attend(qb, qpos, wb[:, :, 0], wb[:, :, 1], kpos)
        return combine(gb, o_c, o_s, o_w)

    o = lax.map(block, jnp.arange(T // Q_BLOCK))
    return o.transpose(1, 0, 2, 3).reshape(B, T, Q_DIM)


def nsa_sample(q, kv_c, kv_s, kv_w, gates, pool_c, pool_s, win_buf, page_table, w1, w2, pe):
    db, tn = q.shape[:2]
    past = page_table.shape[1] * PAGE_SIZE
    past_c = pool_c[page_table].reshape(db, past, 2, KV_GROUPS, HEAD_DIM)
    full_c = jnp.concatenate([past_c, kv_c], axis=1)
    kc = compress(full_c[:, :, 0], w1[0], w2[0], pe[0])
    vc = compress(full_c[:, :, 1], w1[1], w2[1], pe[1])
    ov = overlap_matrix(kc.shape[1], -(-(past + tn) // SEL_BLOCK))
    k_top = min(SEL_TOPN, ov.shape[1])
    qpos = past + jnp.arange(tn)
    o_c, p_c = cmp_attend(q, qpos, kc, vc)
    pos = select_positions(p_c, qpos, ov, k_top)
    kv_sel = paged_gather(pool_s, page_table, kv_s, pos)
    o_s = sel_attend(q, qpos, kv_sel[..., 0, :], kv_sel[..., 1, :], pos)
    wbl = win_buf.shape[1]
    win = jnp.concatenate([win_buf, kv_w], axis=1)
    kpos = past - wbl + jnp.arange(wbl + tn)
    o_w = win_attend(q, qpos, win[:, :, 0], win[:, :, 1], kpos)
    return combine(gates, o_c, o_s, o_w), win[:, tn:]


def gmlp_mix(h, w_in, ln_g, w_s, b_s, w_out):
    B, T, _ = h.shape
    z = jax.nn.gelu(h @ w_in)
    u, v = z[..., :GMLP_DIM], z[..., GMLP_DIM:]
    v = layernorm(v, ln_g)
    pad = (-T) % CHUNK
    n_chunks = (T + pad) // CHUNK
    vc = jnp.pad(v, ((0, 0), (0, pad), (0, 0))).reshape(B, n_chunks, CHUNK, GMLP_GROUPS, GMLP_GW)
    w_m = jnp.where(jnp.tril(jnp.ones((CHUNK, CHUNK), dtype=bool)), w_s, 0)
    mixed = jnp.einsum('gts,bnsgc->bntgc', w_m, vc) + b_s.T[:, :, None]
    mixed = mixed.reshape(B, n_chunks * CHUNK, GMLP_DIM)[:, :T]
    return (u * mixed) @ w_out, v


def _normal(key, shape, scale):
    return jax.random.normal(key, shape, jnp.float32) * scale


def setup_inputs(seed: int = 0) -> dict:
    key = jax.random.key(seed)
    ks = jax.random.split(key, 24)
    n_pages = PAST_LEN // PAGE_SIZE
    n_pool = (DEC_BATCH * n_pages * 5) // 4
    wb = min(WINDOW, PAST_LEN)
    page_table = jax.random.permutation(ks[0], n_pool)[:DEC_BATCH * n_pages]
    page_table = page_table.reshape(DEC_BATCH, n_pages).astype(jnp.int32)
    return {
        'x_prompt': _normal(ks[1], (BATCH, SEQ, D_MODEL), 1.0),
        'x_sample': _normal(ks[2], (DEC_BATCH, DEC_SEQ, D_MODEL), 1.0),
        'cache_cmp_kv': _normal(ks[3], (N_NSA, n_pool, PAGE_SIZE, 2, KV_GROUPS, HEAD_DIM), 1.0),
        'cache_sel_kv': _normal(ks[4], (N_NSA, n_pool, PAGE_SIZE, 2, KV_GROUPS, HEAD_DIM), 1.0),
        'cache_win_kv': _normal(ks[5], (N_NSA, DEC_BATCH, wb, 2, KV_GROUPS, HEAD_DIM), 1.0),
        'page_table': page_table,
        'norm_gains': 1.0 + _normal(ks[6], (DEPTH, 4, D_MODEL), 0.05),
        'w_nsa_in': _normal(ks[7], (N_NSA, D_MODEL, NSA_IN), D_MODEL ** -0.5),
        'w_cmp_hidden': _normal(ks[8], (N_NSA, 2, CMP_BLOCK, HEAD_DIM, CMP_HIDDEN), (CMP_BLOCK * HEAD_DIM) ** -0.5),
        'w_cmp_out': _normal(ks[9], (N_NSA, 2, CMP_HIDDEN, HEAD_DIM), CMP_HIDDEN ** -0.5),
        'cmp_pos_emb': _normal(ks[10], (N_NSA, 2, CMP_BLOCK, HEAD_DIM), 0.5),
        'w_nsa_out': _normal(ks[11], (N_NSA, Q_DIM, D_MODEL), Q_DIM ** -0.5),
        'w_gm_in': _normal(ks[12], (N_GMLP, D_MODEL, 2 * GMLP_DIM), D_MODEL ** -0.5),
        'gm_norm_gain': 1.0 + _normal(ks[13], (N_GMLP, GMLP_DIM), 0.05),
        'w_spatial': _normal(ks[14], (N_GMLP, GMLP_GROUPS, CHUNK, CHUNK), CHUNK ** -0.5),
        'b_spatial': 1.0 + _normal(ks[15], (N_GMLP, GMLP_GROUPS, CHUNK), 0.1),
        'w_gm_out': _normal(ks[16], (N_GMLP, GMLP_DIM, D_MODEL), GMLP_DIM ** -0.5),
        'w_ffn_in': _normal(ks[17], (DEPTH, D_MODEL, 2 * FFN_DIM), D_MODEL ** -0.5),
        'w_ffn_out': _normal(ks[18], (DEPTH, FFN_DIM, D_MODEL), FFN_DIM ** -0.5),
    }


def reference(x_prompt, x_sample, cache_cmp_kv, cache_sel_kv, cache_win_kv, page_table,
              norm_gains, w_nsa_in, w_cmp_hidden, w_cmp_out, cmp_pos_emb, w_nsa_out,
              w_gm_in, gm_norm_gain, w_spatial, b_spatial, w_gm_out, w_ffn_in, w_ffn_out):
    yp, ys = x_prompt, x_sample
    p_cmp, p_sel, p_win, s_cmp, s_sel, s_win, s_v = [], [], [], [], [], [], []
    for i in range(DEPTH):
        hp = rmsnorm(yp, norm_gains[i, 0])
        hs = rmsnorm(ys, norm_gains[i, 0])
        if i % 2 == 0:
            a = i // 2
            qp, kvc_p, kvs_p, kvw_p, gp = nsa_project(hp, w_nsa_in[a])
            op = nsa_prompt(qp, kvc_p, kvs_p, kvw_p, gp, w_cmp_hidden[a], w_cmp_out[a], cmp_pos_emb[a])
            qs, kvc_s, kvs_s, kvw_s, gs = nsa_project(hs, w_nsa_in[a])
            o_s, win_new = nsa_sample(qs, kvc_s, kvs_s, kvw_s, gs, cache_cmp_kv[a], cache_sel_kv[a],
                                      cache_win_kv[a], page_table, w_cmp_hidden[a], w_cmp_out[a],
                                      cmp_pos_emb[a])
            mp = op @ w_nsa_out[a]
            ms = o_s @ w_nsa_out[a]
            B, T = x_prompt.shape[:2]
            page_shape = (B, T // PAGE_SIZE, PAGE_SIZE, 2, KV_GROUPS, HEAD_DIM)
            p_cmp.append(kvc_p.reshape(page_shape))
            p_sel.append(kvs_p.reshape(page_shape))
            p_win.append(kvw_p[:, T - min(WINDOW, T):])
            s_cmp.append(kvc_s)
            s_sel.append(kvs_s)
            s_win.append(win_new)
        else:
            b = i // 2
            mp, _ = gmlp_mix(hp, w_gm_in[b], gm_norm_gain[b], w_spatial[b], b_spatial[b], w_gm_out[b])
            ms, v_new = gmlp_mix(hs, w_gm_in[b], gm_norm_gain[b], w_spatial[b], b_spatial[b], w_gm_out[b])
            s_v.append(v_new)
        yp = yp + rmsnorm(mp, norm_gains[i, 1])
        ys = ys + rmsnorm(ms, norm_gains[i, 1])
        yp = yp + rmsnorm(swiglu(rmsnorm(yp, norm_gains[i, 2]), w_ffn_in[i], w_ffn_out[i]), norm_gains[i, 3])
        ys = ys + rmsnorm(swiglu(rmsnorm(ys, norm_gains[i, 2]), w_ffn_in[i], w_ffn_out[i]), norm_gains[i, 3])
    return (yp, ys, jnp.stack(p_cmp), jnp.stack(p_sel), jnp.stack(p_win),
            jnp.stack(s_cmp), jnp.stack(s_sel), jnp.stack(s_win), jnp.stack(s_v))
```

```python
import functools

import jax
import jax.numpy as jnp
from jax import lax
from jax.experimental import pallas as pl
from jax.experimental.pallas import tpu as pltpu

F32 = jnp.float32
BF16 = jnp.bfloat16

HEAD_DIM = 64
KV_GROUPS = 4
GROUP_SIZE = 4
N_HEADS = KV_GROUPS * GROUP_SIZE
Q_DIM = N_HEADS * HEAD_DIM
GKV = KV_GROUPS * HEAD_DIM
KV_DIM = 2 * GKV
CMP_STRIDE = 16
CMP_BLOCK = 2 * CMP_STRIDE
SEL_BLOCK = 64
SEL_TOPN = 16
N_LOCAL = 2
WINDOW = 512
PAGE_SIZE = 128
CHUNK = 128
GMLP_GROUPS = 8
ATTN_SCALE = HEAD_DIM ** -0.5
EPS = 1e-6
NEG = -1e30
FORCE = 1e4
MASK_BIG = 1e30
LANES = 128
VMEM_LIMIT = 56 * 1024 * 1024


def _params(sem):
    return pltpu.CompilerParams(dimension_semantics=sem, vmem_limit_bytes=VMEM_LIMIT)


def _resident(shape):
    nd = len(shape)
    return pl.BlockSpec(shape, lambda *_: (0,) * nd, pipeline_mode=pl.Buffered(1))


def _rms(x, g):
    return x * lax.rsqrt(jnp.mean(x * x, axis=-1, keepdims=True) + EPS) * g


def _dot(a, b):
    return jnp.dot(a, b, preferred_element_type=F32)


def _dot_nt(a, b):
    return lax.dot_general(a, b, (((1,), (1,)), ((), ())), preferred_element_type=F32)


def _post_kernel(m_ref, x_ref, wo_ref, g_ref, win_ref, wout_ref, y_ref, *, ffn, ck):
    x = x_ref[...]
    y1 = x + _rms(_dot(m_ref[...], wo_ref[...]), g_ref[1:2, :])
    h = _rms(y1, g_ref[2:3, :]).astype(BF16)
    acc = jnp.zeros_like(x)
    for c0 in range(0, ffn, ck):
        w = min(ck, ffn - c0)
        a = _dot(h, win_ref[:, c0:c0 + w])
        b = _dot(h, win_ref[:, ffn + c0:ffn + c0 + w])
        act = (jax.nn.silu(a) * b).astype(BF16)
        acc = acc + _dot(act, wout_ref[c0:c0 + w, :])
    y_ref[...] = y1 + _rms(acc, g_ref[3:4, :])


def _post(m, x, wo, gains, w_in, w_out, tm):
    rows, d = x.shape
    km = m.shape[1]
    ffn = w_out.shape[0]
    return pl.pallas_call(
        functools.partial(_post_kernel, ffn=ffn, ck=256),
        grid=(rows // tm,),
        in_specs=[
            pl.BlockSpec((tm, km), lambda i: (i, 0)),
            pl.BlockSpec((tm, d), lambda i: (i, 0)),
            _resident(wo.shape),
            _resident(gains.shape),
            _resident(w_in.shape),
            _resident(w_out.shape),
        ],
        out_specs=pl.BlockSpec((tm, d), lambda i: (i, 0)),
        out_shape=jax.ShapeDtypeStruct((rows, d), F32),
        compiler_params=_params(("parallel",)),
        name="post_mixer",
    )(m, x, wo, gains, w_in, w_out)


def _gmlp_kernel(x_ref, g_ref, win_ref, lng_ref, wm_ref, bs_ref, t_ref, *v_ref, nchunk, gd):
    h = _rms(x_ref[...], g_ref[0:1, :]).astype(BF16)
    u = jax.nn.gelu(_dot(h, win_ref[:, :gd]))
    vp = jax.nn.gelu(_dot(h, win_ref[:, gd:]))
    xc = vp - jnp.mean(vp, axis=-1, keepdims=True)
    v = xc * lax.rsqrt(jnp.mean(xc * xc, axis=-1, keepdims=True) + EPS) * lng_ref[...]
    if v_ref:
        v_ref[0][...] = v
    vb = v.astype(BF16)
    gw = gd // GMLP_GROUPS
    tri = (lax.broadcasted_iota(jnp.int32, (CHUNK, CHUNK), 0)
           >= lax.broadcasted_iota(jnp.int32, (CHUNK, CHUNK), 1))
    for gg in range(GMLP_GROUPS):
        wmg = jnp.where(tri, wm_ref[gg], 0.0).astype(BF16)
        bias = bs_ref[:, gg:gg + 1]
        for c in range(nchunk):
            rs = slice(c * CHUNK, (c + 1) * CHUNK)
            cs = slice(gg * gw, (gg + 1) * gw)
            mixed = _dot(wmg, vb[rs, cs]) + bias
            t_ref[rs, cs] = (u[rs, cs] * mixed).astype(BF16)


def _gmlp(x, gains, w_in, ln_g, wm, bs_t, tm, emit_v):
    rows, d = x.shape
    gd = w_in.shape[1] // 2
    out_shape = [jax.ShapeDtypeStruct((rows, gd), BF16)]
    out_specs = [pl.BlockSpec((tm, gd), lambda i: (i, 0))]
    if emit_v:
        out_shape.append(jax.ShapeDtypeStruct((rows, gd), F32))
        out_specs.append(pl.BlockSpec((tm, gd), lambda i: (i, 0)))
    return pl.pallas_call(
        functools.partial(_gmlp_kernel, nchunk=tm // CHUNK, gd=gd),
        grid=(rows // tm,),
        in_specs=[
            pl.BlockSpec((tm, d), lambda i: (i, 0)),
            _resident(gains.shape),
            _resident(w_in.shape),
            _resident(ln_g.shape),
            _resident(wm.shape),
            _resident(bs_t.shape),
        ],
        out_specs=out_specs,
        out_shape=out_shape,
        compiler_params=_params(("parallel",)),
        name="gmlp_mix",
    )(x, gains, w_in, ln_g, wm, bs_t)


def _nsa_proj_kernel(x_ref, g_ref, wq_ref, wkv_ref, wg_ref,
                     q_ref, kvc_ref, kvs_ref, kvw_ref, gate_ref,
                     ksel_ref, vsel_ref, kwin_ref, vwin_ref, *, tm, seq):
    h = _rms(x_ref[...], g_ref[0:1, :]).astype(BF16)
    q_ref[...] = (_dot(h, wq_ref[...]) * ATTN_SCALE).astype(BF16)
    kv = _dot(h, wkv_ref[...])
    kvc_ref[...] = kv[:, :KV_DIM]
    kvs = kv[:, KV_DIM:2 * KV_DIM]
    kvw = kv[:, 2 * KV_DIM:]
    kvs_ref[...] = kvs
    kvw_ref[...] = kvw
    gz = _dot(h, wg_ref[...])
    for g in range(KV_GROUPS):
        gate_ref[g] = jax.nn.sigmoid(gz[:, g * LANES:(g + 1) * LANES])
    t0 = lax.rem(pl.program_id(0) * tm, seq)
    blk = (t0 + lax.broadcasted_iota(jnp.int32, (tm, LANES), 0)) // SEL_BLOCK
    onehot = jnp.where(lax.broadcasted_iota(jnp.int32, (tm, LANES), 1) == blk,
                       -MASK_BIG, 0.0).astype(BF16)
    ones = jnp.ones((tm, HEAD_DIM), BF16)
    for g in range(KV_GROUPS):
        ks = slice(g * HEAD_DIM, (g + 1) * HEAD_DIM)
        vs = slice(GKV + g * HEAD_DIM, GKV + (g + 1) * HEAD_DIM)
        ksel_ref[g, :, 0:LANES] = onehot
        ksel_ref[g, :, LANES:LANES + HEAD_DIM] = kvs[:, ks].astype(BF16)
        vsel_ref[g, :, 0:HEAD_DIM] = kvs[:, vs].astype(BF16)
        vsel_ref[g, :, HEAD_DIM:] = ones
        kwin_ref[g] = kvw[:, ks].astype(BF16)
        vwin_ref[g, :, 0:HEAD_DIM] = kvw[:, vs].astype(BF16)
        vwin_ref[g, :, HEAD_DIM:] = ones


def _nsa_proj(x, gains, wq, wkv, wg, tm, seq):
    rows, d = x.shape
    row = lambda i: (i, 0)
    grow = lambda i: (0, i, 0)
    kaug = LANES + HEAD_DIM
    out_shape = [
        jax.ShapeDtypeStruct((rows, Q_DIM), BF16),
        jax.ShapeDtypeStruct((rows, KV_DIM), F32),
        jax.ShapeDtypeStruct((rows, KV_DIM), F32),
        jax.ShapeDtypeStruct((rows, KV_DIM), F32),
        jax.ShapeDtypeStruct((KV_GROUPS, rows, LANES), F32),
        jax.ShapeDtypeStruct((KV_GROUPS, rows, kaug), BF16),
        jax.ShapeDtypeStruct((KV_GROUPS, rows, LANES), BF16),
        jax.ShapeDtypeStruct((KV_GROUPS, rows, HEAD_DIM), BF16),
        jax.ShapeDtypeStruct((KV_GROUPS, rows, LANES), BF16),
    ]
    out_specs = [
        pl.BlockSpec((tm, Q_DIM), row),
        pl.BlockSpec((tm, KV_DIM), row),
        pl.BlockSpec((tm, KV_DIM), row),
        pl.BlockSpec((tm, KV_DIM), row),
        pl.BlockSpec((KV_GROUPS, tm, LANES), grow),
        pl.BlockSpec((KV_GROUPS, tm, kaug), grow),
        pl.BlockSpec((KV_GROUPS, tm, LANES), grow),
        pl.BlockSpec((KV_GROUPS, tm, HEAD_DIM), grow),
        pl.BlockSpec((KV_GROUPS, tm, LANES), grow),
    ]
    return pl.pallas_call(
        functools.partial(_nsa_proj_kernel, tm=tm, seq=seq),
        grid=(rows // tm,),
        in_specs=[
            pl.BlockSpec((tm, d), row),
            _resident(gains.shape),
            _resident(wq.shape),
            _resident(wkv.shape),
            _resident(wg.shape),
        ],
        out_specs=out_specs,
        out_shape=out_shape,
        compiler_params=_params(("parallel",)),
        name="nsa_proj",
    )(x, gains, wq, wkv, wg)


def _compress_ab_compute(xget, bd_ref, ab_ref):
    row_w = KV_DIM
    for kv in range(2):
        acc = None
        for l in range(CMP_STRIDE):
            xl = xget(l * row_w + kv * GKV, GKV).astype(BF16)
            part = _dot(xl, bd_ref[kv, l])
            acc = part if acc is None else acc + part
        ab_ref[:, kv * 2 * GKV:(kv + 1) * 2 * GKV] = acc


def _compress_ab_kernel(x_ref, bd_ref, ab_ref):
    _compress_ab_compute(lambda c0, w: x_ref[:, c0:c0 + w], bd_ref, ab_ref)


def _compress_ab(x, bd):
    nb, n, width = x.shape
    rb = min(n, 256)
    return pl.pallas_call(
        _compress_ab_kernel,
        grid=(nb, n // rb),
        in_specs=[pl.BlockSpec((None, rb, width), lambda b, r: (b, r, 0)), _resident(bd.shape)],
        out_specs=pl.BlockSpec((None, rb, 4 * GKV), lambda b, r: (b, r, 0)),
        out_shape=jax.ShapeDtypeStruct((nb, n, 4 * GKV), F32),
        compiler_params=_params(("parallel", "parallel")),
        name="compress_ab",
    )(x, bd)


def _compress_ab_paged_kernel(pt_ref, pool_ref, bd_ref, ab_ref, xbuf, sem, *, pgs, nsplit):
    s = pl.program_id(0)
    nsteps = pl.num_programs(0)
    rows_pp = PAGE_SIZE // CMP_STRIDE

    def copies(step, slot):
        b = step // nsplit
        h = step % nsplit
        return [pltpu.make_async_copy(pool_ref.at[pt_ref[b, h * pgs + p]],
                                      xbuf.at[slot, pl.ds(p * rows_pp, rows_pp), :],
                                      sem.at[slot]) for p in range(pgs)]

    @pl.when(s == 0)
    def _():
        for c in copies(s, 0):
            c.start()

    @pl.when(s + 1 < nsteps)
    def _():
        for c in copies(s + 1, (s + 1) % 2):
            c.start()

    slot = s % 2
    for c in copies(s, slot):
        c.wait()
    _compress_ab_compute(lambda c0, w: xbuf[slot, :, pl.ds(c0, w)], bd_ref, ab_ref)


def _compress_ab_paged(pt, pool, bd, pgs):
    nb, n_pages = pt.shape
    nsplit = n_pages // pgs
    rows_pp = PAGE_SIZE // CMP_STRIDE
    rb = pgs * rows_pp
    width = pool.shape[2]
    grid_spec = pltpu.PrefetchScalarGridSpec(
        num_scalar_prefetch=1,
        grid=(nb * nsplit,),
        in_specs=[pl.BlockSpec(memory_space=pl.ANY),
                  pl.BlockSpec(bd.shape, lambda s, pt: (0,) * 4, pipeline_mode=pl.Buffered(1))],
        out_specs=pl.BlockSpec((None, rb, 4 * GKV), lambda s, pt: (s // nsplit, s % nsplit, 0)),
        scratch_shapes=[pltpu.VMEM((2, rb, width), F32), pltpu.SemaphoreType.DMA((2,))],
    )
    return pl.pallas_call(
        functools.partial(_compress_ab_paged_kernel, pgs=pgs, nsplit=nsplit),
        grid_spec=grid_spec,
        out_shape=jax.ShapeDtypeStruct((nb, n_pages * rows_pp, 4 * GKV), F32),
        compiler_params=_params(("arbitrary",)),
        name="compress_ab_paged",
    )(pt, pool, bd)


def _compress_fin_kernel(ab_ref, pe_ref, w1_ref, w2_ref, kc_ref, vc_ref, *, n):
    for kv, out_ref in ((0, kc_ref), (1, vc_ref)):
        a = ab_ref[:, kv * 2 * GKV:kv * 2 * GKV + GKV]
        b = ab_ref[:, kv * 2 * GKV + GKV:(kv + 1) * 2 * GKV]
        c = _dot(pe_ref[kv], w1_ref[kv])[0:1, :]
        c4 = jnp.concatenate([c] * KV_GROUPS, axis=1)
        hid = a + pltpu.roll(b, n - 1, 0) + c4
        o = _dot(jax.nn.gelu(hid).astype(BF16), w2_ref[kv])
        for g in range(KV_GROUPS):
            out_ref[g] = o[:, g * HEAD_DIM:(g + 1) * HEAD_DIM].astype(BF16)


def _compress_fin(ab, pe8, w1f, w2bd):
    nb, n, _ = ab.shape
    out = jax.ShapeDtypeStruct((nb, KV_GROUPS, n, HEAD_DIM), BF16)
    ospec = pl.BlockSpec((None, KV_GROUPS, n, HEAD_DIM), lambda b: (b, 0, 0, 0))
    return pl.pallas_call(
        functools.partial(_compress_fin_kernel, n=n),
        grid=(nb,),
        in_specs=[pl.BlockSpec((None, n, 4 * GKV), lambda b: (b, 0, 0)),
                  _resident(pe8.shape), _resident(w1f.shape), _resident(w2bd.shape)],
        out_specs=[ospec, ospec],
        out_shape=[out, out],
        compiler_params=_params(("parallel",)),
        name="compress_fin",
    )(ab, pe8, w1f, w2bd)


def _group_rows(qt):
    return jnp.concatenate([qt[:, r * HEAD_DIM:(r + 1) * HEAD_DIM] for r in range(GROUP_SIZE)],
                           axis=0)


def _ungroup_rows(o, tq):
    return jnp.concatenate([o[r * tq:(r + 1) * tq] for r in range(GROUP_SIZE)], axis=1)


def _row_qpos(s0, tq):
    q = lax.broadcasted_iota(jnp.int32, (tq, 1), 0)
    return s0 + jnp.concatenate([q] * GROUP_SIZE, axis=0)


def _gate_cols(gt, j, tq):
    return jnp.concatenate([gt[:, 3 * r + j:3 * r + j + 1] for r in range(GROUP_SIZE)], axis=0)


def _topk_mask(score, k):
    nb = score.shape[0]
    idx = lax.broadcasted_iota(jnp.int32, score.shape, 0).astype(F32)

    def body(_, carry):
        work, sel = carry
        m = jnp.max(work, axis=0, keepdims=True)
        first = jnp.min(jnp.where(work == m, idx, float(nb)), axis=0, keepdims=True)
        pick = idx == first
        return jnp.where(pick, -3e38, work), jnp.where(pick, 1.0, sel)

    _, sel = lax.fori_loop(0, k, body, (score, jnp.zeros_like(score)), unroll=True)
    return sel


def _cmp_select_kernel(q_ref, kc_ref, vc_ref, ovt_ref, gate_ref, oc_ref, nsel_ref,
                       *, tq, pos0, k_top):
    s0 = pos0 + pl.program_id(2) * tq
    qrows = _group_rows(q_ref[...])
    n = kc_ref.shape[0]
    s = _dot_nt(qrows, kc_ref[...])
    qpos = _row_qpos(s0, tq)
    end = lax.broadcasted_iota(jnp.int32, (1, n), 1) * CMP_STRIDE + (CMP_BLOCK - 1)
    sm = jnp.where(end <= qpos, s, NEG)
    e = jnp.exp(sm - jnp.max(sm, axis=-1, keepdims=True))
    p = e / jnp.sum(e, axis=-1, keepdims=True)
    p = jnp.where(qpos >= CMP_BLOCK - 1, p, 0.0)
    oc = _dot(p.astype(BF16), vc_ref[...])
    gc = _gate_cols(gate_ref[...], 0, tq)
    oc_ref[...] = _ungroup_rows(gc * oc, tq).astype(BF16)
    psum = p[0:tq]
    for r in range(1, GROUP_SIZE):
        psum = psum + p[r * tq:(r + 1) * tq]
    hi = psum.astype(BF16)
    lo = (psum - hi.astype(F32)).astype(BF16)
    imp_t = _dot_nt(ovt_ref[...], hi) + _dot_nt(ovt_ref[...], lo)
    shape = imp_t.shape
    blk = lax.broadcasted_iota(jnp.int32, shape, 0)
    cur = (s0 + lax.broadcasted_iota(jnp.int32, shape, 1)) // SEL_BLOCK
    valid = blk <= cur
    forced = (blk == 0) | (blk > cur - N_LOCAL)
    score = jnp.where(valid, jnp.where(forced, FORCE, imp_t), NEG)
    sel = _topk_mask(score, k_top)
    nsel_t = jnp.where((sel > 0.5) & valid, 0.0, 1.0)
    nsel_ref[...] = nsel_t.T.astype(BF16)


def _cmp_select(q, kc, vc, ovt, gates, tpad, pos0, k_top):
    nb, _, n, _ = kc.shape
    tq = 128
    nq = tpad // tq
    nbp = ovt.shape[0]
    gq = GROUP_SIZE * HEAD_DIM
    return pl.pallas_call(
        functools.partial(_cmp_select_kernel, tq=tq, pos0=pos0, k_top=k_top),
        grid=(nb, KV_GROUPS, nq),
        in_specs=[
            pl.BlockSpec((tq, gq), lambda b, g, i: (b * nq + i, g)),
            pl.BlockSpec((None, None, n, HEAD_DIM), lambda b, g, i: (b, g, 0, 0)),
            pl.BlockSpec((None, None, n, HEAD_DIM), lambda b, g, i: (b, g, 0, 0)),
            pl.BlockSpec(ovt.shape, lambda b, g, i: (0, 0)),
            pl.BlockSpec((None, tq, LANES), lambda b, g, i: (g, b * nq + i, 0)),
        ],
        out_specs=[
            pl.BlockSpec((tq, gq), lambda b, g, i: (b * nq + i, g)),
            pl.BlockSpec((None, None, tq, nbp), lambda b, g, i: (b, g, i, 0)),
        ],
        out_shape=[
            jax.ShapeDtypeStruct((nb * tpad, Q_DIM), BF16),
            jax.ShapeDtypeStruct((nb, KV_GROUPS, tpad, nbp), BF16),
        ],
        compiler_params=_params(("parallel", "parallel", "parallel")),
        name="cmp_select",
    )(q, kc, vc, ovt, gates)


def _sel_win_kernel(q_ref, nsel_ref, gate_ref, oc_ref, ksel_ref, vsel_ref, kwin_ref, vwin_ref,
                    o_ref, *, tq, tk, seq):
    s0 = pl.program_id(2) * tq
    rows = GROUP_SIZE * tq
    qrows = _group_rows(q_ref[...])
    ns = nsel_ref[...]
    qaug = jnp.concatenate([jnp.concatenate([ns] * GROUP_SIZE, axis=0), qrows], axis=1)
    qpos = _row_qpos(s0, tq)
    nkv = (s0 + tq + tk - 1) // tk

    def body(j, carry):
        m, acc = carry
        k0 = pl.multiple_of(j * tk, tk)
        s = _dot_nt(qaug, ksel_ref[pl.ds(k0, tk), :])
        kpos = k0 + lax.broadcasted_iota(jnp.int32, (1, tk), 1)
        s = jnp.where(kpos <= qpos, s, NEG)
        mn = jnp.maximum(m, jnp.max(s, axis=-1, keepdims=True))
        p = jnp.exp(s - mn)
        acc = jnp.exp(m - mn) * acc + _dot(p.astype(BF16), vsel_ref[pl.ds(k0, tk), :])
        return mn, acc

    m0 = jnp.full((rows, 1), NEG, F32)
    _, acc = lax.fori_loop(0, nkv, body, (m0, jnp.zeros((rows, LANES), F32)))
    o_s = acc[:, :HEAD_DIM] / acc[:, HEAD_DIM:HEAD_DIM + 1]

    nw = min(WINDOW + tq, seq)
    w0 = pl.multiple_of(jnp.maximum(s0 + tq - nw, 0), tq)
    sw = _dot_nt(qrows, kwin_ref[pl.ds(w0, nw), :])
    d = qpos - (w0 + lax.broadcasted_iota(jnp.int32, (1, nw), 1))
    sw = jnp.where((d >= 0) & (d < WINDOW), sw, NEG)
    pw = jnp.exp(sw - jnp.max(sw, axis=-1, keepdims=True))
    accw = _dot(pw.astype(BF16), vwin_ref[pl.ds(w0, nw), :])
    o_w = accw[:, :HEAD_DIM] / accw[:, HEAD_DIM:HEAD_DIM + 1]

    gt = gate_ref[...]
    o = _gate_cols(gt, 1, tq) * o_s + _gate_cols(gt, 2, tq) * o_w
    o_ref[...] = (oc_ref[...].astype(F32) + _ungroup_rows(o, tq)).astype(BF16)


def _sel_win(q, nsel, gates, ocg, ksel, vsel, kwin, vwin, nb, seq):
    tq = 128
    tk = min(512, seq)
    nq = seq // tq
    gq = GROUP_SIZE * HEAD_DIM
    qspec = pl.BlockSpec((tq, gq), lambda b, g, i: (b * nq + i, g))
    kv = lambda w: pl.BlockSpec((None, seq, w), lambda b, g, i: (g, b, 0))
    return pl.pallas_call(
        functools.partial(_sel_win_kernel, tq=tq, tk=tk, seq=seq),
        grid=(nb, KV_GROUPS, nq),
        in_specs=[
            qspec,
            pl.BlockSpec((None, None, tq, LANES), lambda b, g, i: (b, g, i, 0)),
            pl.BlockSpec((None, tq, LANES), lambda b, g, i: (g, b * nq + i, 0)),
            qspec,
            kv(LANES + HEAD_DIM), kv(LANES), kv(HEAD_DIM), kv(LANES),
        ],
        out_specs=qspec,
        out_shape=jax.ShapeDtypeStruct((nb * seq, Q_DIM), BF16),
        compiler_params=_params(("parallel", "parallel", "arbitrary")),
        name="sel_win_attn",
    )(q, nsel, gates, ocg, ksel, vsel, kwin, vwin)


def _sample_attn_kernel(pt_ref, pool_ref, qbd_ref, nselr_ref, eneg_ref, knew_ref, wnew_ref,
                        wbuf_ref, gate_ref, ocg_ref, o_ref,
                        kbuf, sem, m_sc, l_sc, acc_sc, *, pgs, nch, tk, past, tn):
    b = pl.program_id(0)
    c = pl.program_id(1)
    step = b * nch + c
    nsteps = pl.num_programs(0) * nch

    def copies(st, slot):
        bb = st // nch
        cc = st % nch
        return [pltpu.make_async_copy(pool_ref.at[pt_ref[bb, cc * pgs + p]],
                                      kbuf.at[slot, pl.ds(p * PAGE_SIZE, PAGE_SIZE), :],
                                      sem.at[slot]) for p in range(pgs)]

    @pl.when(step == 0)
    def _():
        for cp in copies(step, 0):
            cp.start()

    @pl.when(step + 1 < nsteps)
    def _():
        for cp in copies(step + 1, (step + 1) % 2):
            cp.start()

    @pl.when(c == 0)
    def _():
        m_sc[...] = jnp.full(m_sc.shape, NEG, F32)
        l_sc[...] = jnp.zeros(l_sc.shape, F32)
        acc_sc[...] = jnp.zeros(acc_sc.shape, F32)

    slot = step % 2
    for cp in copies(step, slot):
        cp.wait()

    qbd = qbd_ref[...]
    nselr = nselr_ref[...]
    rows = qbd.shape[0]
    qpos = past + lax.rem(lax.broadcasted_iota(jnp.int32, (rows, 1), 0), tn)

    def online(s, v):
        m = m_sc[...]
        mn = jnp.maximum(m, jnp.max(s, axis=-1, keepdims=True))
        p = jnp.exp(s - mn)
        alpha = jnp.exp(m - mn)
        l_sc[...] = alpha * l_sc[...] + jnp.sum(p, axis=-1, keepdims=True)
        acc_sc[...] = alpha * acc_sc[...] + _dot(p.astype(BF16), v)
        m_sc[...] = mn

    for t in range(pgs * PAGE_SIZE // tk):
        k = kbuf[slot, pl.ds(t * tk, tk), pl.ds(0, GKV)].astype(BF16)
        v = kbuf[slot, pl.ds(t * tk, tk), pl.ds(GKV, GKV)].astype(BF16)
        k0 = pl.multiple_of(c * (pgs * PAGE_SIZE) + t * tk, tk)
        bias = _dot_nt(nselr, eneg_ref[pl.ds(k0, tk), :])
        online(_dot_nt(qbd, k) + bias, v)

    @pl.when(c == nch - 1)
    def _():
        npad = knew_ref.shape[0]
        newpos = past + lax.broadcasted_iota(jnp.int32, (1, npad), 1)
        new_ok = (newpos <= qpos) & (newpos < past + tn)
        kn = knew_ref[:, 0:GKV].astype(BF16)
        vn = knew_ref[:, GKV:].astype(BF16)
        online(jnp.where(new_ok, _dot_nt(qbd, kn), NEG), vn)
        o_s = acc_sc[...] / l_sc[...]
        wbl = wbuf_ref.shape[0]
        kw = wbuf_ref[:, 0:GKV].astype(BF16)
        vw = wbuf_ref[:, GKV:].astype(BF16)
        dw = qpos - (past - wbl + lax.broadcasted_iota(jnp.int32, (1, wbl), 1))
        s1 = jnp.where((dw >= 0) & (dw < WINDOW), _dot_nt(qbd, kw), NEG)
        s2 = jnp.where(new_ok & (qpos - newpos < WINDOW),
                       _dot_nt(qbd, wnew_ref[:, 0:GKV].astype(BF16)), NEG)
        mw = jnp.maximum(jnp.max(s1, axis=-1, keepdims=True), jnp.max(s2, axis=-1, keepdims=True))
        p1 = jnp.exp(s1 - mw)
        p2 = jnp.exp(s2 - mw)
        lw = jnp.sum(p1, axis=-1, keepdims=True) + jnp.sum(p2, axis=-1, keepdims=True)
        o_w = (_dot(p1.astype(BF16), vw)
               + _dot(p2.astype(BF16), wnew_ref[:, GKV:].astype(BF16))) / lw
        gt = gate_ref[...]
        o_ref[...] = ocg_ref[...] + gt[:, 1:2] * o_s + gt[:, 2:3] * o_w


def _sample_attn(pt, pool, qbd, nselr, eneg, knew, wnew, wbuf, wbuf_off, gates_r, ocg_r,
                 pgs, past, tn):
    nb, n_pages = pt.shape
    nch = n_pages // pgs
    rows = qbd.shape[1]
    tk = 512
    per_b = lambda shape: pl.BlockSpec((None,) + shape, lambda b, c, pt: (b, 0, 0))
    grid_spec = pltpu.PrefetchScalarGridSpec(
        num_scalar_prefetch=1,
        grid=(nb, nch),
        in_specs=[
            pl.BlockSpec(memory_space=pl.ANY),
            per_b(qbd.shape[1:]),
            per_b(nselr.shape[1:]),
            pl.BlockSpec(eneg.shape, lambda b, c, pt: (0, 0), pipeline_mode=pl.Buffered(1)),
            per_b(knew.shape[1:]),
            per_b(wnew.shape[1:]),
            pl.BlockSpec((None,) + wbuf.shape[1:], lambda b, c, pt: (wbuf_off + b, 0, 0)),
            per_b(gates_r.shape[1:]),
            per_b(ocg_r.shape[1:]),
        ],
        out_specs=per_b((rows, GKV)),
        scratch_shapes=[
            pltpu.VMEM((2, pgs * PAGE_SIZE, KV_DIM), F32),
            pltpu.SemaphoreType.DMA((2,)),
            pltpu.VMEM((rows, 1), F32),
            pltpu.VMEM((rows, 1), F32),
            pltpu.VMEM((rows, GKV), F32),
        ],
    )
    return pl.pallas_call(
        functools.partial(_sample_attn_kernel, pgs=pgs, nch=nch, tk=tk, past=past, tn=tn),
        grid_spec=grid_spec,
        out_shape=jax.ShapeDtypeStruct((nb, rows, GKV), F32),
        compiler_params=_params(("arbitrary", "arbitrary")),
        name="sample_attn",
    )(pt, pool, qbd, nselr, eneg, knew, wnew, wbuf, gates_r, ocg_r)


def _overlap_t(nbp, nrow):
    cs = jnp.arange(nrow)[None, :] * CMP_STRIDE
    ss = jnp.arange(nbp)[:, None] * SEL_BLOCK
    ov = jnp.minimum(cs + CMP_BLOCK, ss + SEL_BLOCK) - jnp.maximum(cs, ss)
    return (jnp.clip(ov, 0).astype(F32) / CMP_BLOCK).astype(BF16)


def _block_diag_w1(w1):
    eye = jnp.eye(KV_GROUPS, dtype=w1.dtype)
    bd = jnp.einsum('gh,klde->klgdhe', eye, w1).reshape(2, CMP_BLOCK, GKV, GKV)
    return jnp.concatenate([bd[:, :CMP_STRIDE], bd[:, CMP_STRIDE:]], axis=-1).astype(BF16)


def _block_diag_w2(w2):
    eye = jnp.eye(KV_GROUPS, dtype=w2.dtype)
    return jnp.einsum('gh,kde->kgdhe', eye, w2).reshape(2, GKV, GKV).astype(BF16)


def _gate_weight(wg):
    d = wg.shape[0]
    w = wg.reshape(d, KV_GROUPS, 3 * GROUP_SIZE)
    w = jnp.pad(w, ((0, 0), (0, 0), (0, LANES - 3 * GROUP_SIZE)))
    return w.reshape(d, KV_GROUPS * LANES).astype(BF16)


def _nsa_layer(yp, ys, a, cache_cmp_kv, cache_sel_kv, cache_win_kv, page_table, gains,
               w_nsa_in, w_cmp_hidden, w_cmp_out, cmp_pos_emb, dims):
    bsz, seq, db, tn, past = dims
    n_pool = cache_cmp_kv.shape[1]
    n_pages = page_table.shape[1]
    w_in = w_nsa_in[a]
    wq = w_in[:, :Q_DIM].astype(BF16)
    wkv = w_in[:, Q_DIM:Q_DIM + 3 * KV_DIM].astype(BF16)
    wg = _gate_weight(w_in[:, Q_DIM + 3 * KV_DIM:])
    bd1 = _block_diag_w1(w_cmp_hidden[a])
    w2bd = _block_diag_w2(w_cmp_out[a])
    w1f = w_cmp_hidden[a].reshape(2, CMP_BLOCK * HEAD_DIM, HEAD_DIM).astype(BF16)
    pe8 = jnp.broadcast_to(cmp_pos_emb[a].reshape(2, 1, CMP_BLOCK * HEAD_DIM),
                           (2, 8, CMP_BLOCK * HEAD_DIM)).astype(BF16)

    tm = min(512, seq)
    qp, kvc_p, kvs_p, kvw_p, gate_p, ksel, vsel, kwin, vwin = _nsa_proj(
        yp, gains, wq, wkv, wg, tm, seq)
    nrow = seq // CMP_STRIDE
    ab = _compress_ab(kvc_p.reshape(bsz, nrow, CMP_STRIDE * KV_DIM), bd1)
    kc, vc = _compress_fin(ab, pe8, w1f, w2bd)
    ns_p = -(-seq // SEL_BLOCK)
    assert ns_p <= LANES
    ocg, nsel = _cmp_select(qp, kc, vc, _overlap_t(LANES, nrow), gate_p, seq, 0,
                            min(SEL_TOPN, ns_p))
    op = _sel_win(qp, nsel, gate_p, ocg, ksel, vsel, kwin, vwin, bsz, seq)

    rows_s = db * tn
    qs, kvc_s, kvs_s, kvw_s, gate_s, _, _, _, _ = _nsa_proj(ys, gains, wq, wkv, wg, rows_s, rows_s)
    pt_abs = page_table + a * n_pool
    pgs_c = min(32, n_pages)
    pool_c = cache_cmp_kv.reshape(-1, PAGE_SIZE // CMP_STRIDE, CMP_STRIDE * KV_DIM)
    ab_s = _compress_ab_paged(pt_abs, pool_c, bd1, pgs_c)
    kc_s, vc_s = _compress_fin(ab_s, pe8, w1f, w2bd)
    nrow_s = past // CMP_STRIDE
    ns_s = -(-(past + tn) // SEL_BLOCK)
    nbp_s = -(-ns_s // LANES) * LANES
    tpad = 128
    qs_pad = jnp.pad(qs.reshape(db, tn, Q_DIM), ((0, 0), (0, tpad - tn), (0, 0)))
    gate_pad = jnp.pad(gate_s.reshape(KV_GROUPS, db, tn, LANES),
                       ((0, 0), (0, 0), (0, tpad - tn), (0, 0)))
    ocg_s, nsel_s = _cmp_select(qs_pad.reshape(db * tpad, Q_DIM), kc_s, vc_s,
                                _overlap_t(nbp_s, nrow_s),
                                gate_pad.reshape(KV_GROUPS, db * tpad, LANES), tpad, past,
                                min(SEL_TOPN, ns_s))
    eye = jnp.eye(KV_GROUPS, dtype=BF16)
    q5 = qs.reshape(db, tn, KV_GROUPS, GROUP_SIZE, HEAD_DIM)
    rows = KV_GROUPS * GROUP_SIZE * tn
    qbd = jnp.einsum('bqgrd,gh->bgrqhd', q5, eye).reshape(db, rows, GKV)
    nselr = jnp.broadcast_to(nsel_s[:, :, None, :tn, :LANES],
                             (db, KV_GROUPS, GROUP_SIZE, tn, LANES)).reshape(db, rows, LANES)
    g4 = gate_s.reshape(KV_GROUPS, db, tn, LANES)[..., :3 * GROUP_SIZE]
    g4 = g4.reshape(KV_GROUPS, db, tn, GROUP_SIZE, 3).transpose(1, 0, 3, 2, 4)
    gates_r = jnp.pad(g4.reshape(db, rows, 3), ((0, 0), (0, 0), (0, LANES - 3)))
    oc5 = ocg_s.reshape(db, tpad, KV_GROUPS, GROUP_SIZE, HEAD_DIM)[:, :tn].astype(F32)
    ocg_r = jnp.einsum('bqgrd,gh->bgrqhd', oc5, jnp.eye(KV_GROUPS, dtype=F32)).reshape(
        db, rows, GKV)
    kpos = jnp.arange(past)[:, None] // SEL_BLOCK
    eneg = jnp.where(kpos == jnp.arange(LANES)[None, :], -MASK_BIG, 0.0).astype(BF16)
    npad = 128
    knew = jnp.pad(kvs_s.reshape(db, tn, KV_DIM), ((0, 0), (0, npad - tn), (0, 0)))
    wnew = jnp.pad(kvw_s.reshape(db, tn, KV_DIM), ((0, 0), (0, npad - tn), (0, 0)))
    wbl = cache_win_kv.shape[2]
    wbuf = cache_win_kv.reshape(-1, wbl, KV_DIM)
    pool_s = cache_sel_kv.reshape(-1, PAGE_SIZE, KV_DIM)
    o_rows = _sample_attn(pt_abs, pool_s, qbd, nselr, eneg, knew, wnew, wbuf, a * db,
                          gates_r, ocg_r, min(16, n_pages), past, tn)
    o6 = o_rows.reshape(db, KV_GROUPS, GROUP_SIZE, tn, KV_GROUPS, HEAD_DIM)
    o_s = jnp.einsum('bgrqhd,gh->bqgrd', o6, jnp.eye(KV_GROUPS, dtype=F32))
    o_s = o_s.reshape(rows_s, Q_DIM).astype(BF16)

    shp = (KV_GROUPS, HEAD_DIM)
    page_shape = (bsz, seq // PAGE_SIZE, PAGE_SIZE, 2) + shp
    kvw_p5 = kvw_p.reshape((bsz, seq, 2) + shp)
    kvw_s5 = kvw_s.reshape((db, tn, 2) + shp)
    win_new = jnp.concatenate([cache_win_kv[a], kvw_s5], axis=1)[:, tn:]
    caches = (kvc_p.reshape(page_shape), kvs_p.reshape(page_shape),
              kvw_p5[:, seq - min(WINDOW, seq):],
              kvc_s.reshape((db, tn, 2) + shp), kvs_s.reshape((db, tn, 2) + shp), win_new)
    return op, o_s, caches


def kernel(x_prompt, x_sample, cache_cmp_kv, cache_sel_kv, cache_win_kv, page_table, norm_gains,
           w_nsa_in, w_cmp_hidden, w_cmp_out, cmp_pos_emb, w_nsa_out, w_gm_in, gm_norm_gain,
           w_spatial, b_spatial, w_gm_out, w_ffn_in, w_ffn_out):
    bsz, seq, d = x_prompt.shape
    db, tn, _ = x_sample.shape
    past = page_table.shape[1] * PAGE_SIZE
    depth = norm_gains.shape[0]
    assert seq % CHUNK == 0 and past % SEL_BLOCK == 0 and tn < CMP_STRIDE
    assert (db * tn) % 8 == 0 and CHUNK % tn == 0 and db * tn == CHUNK
    dims = (bsz, seq, db, tn, past)
    yp = x_prompt.reshape(bsz * seq, d)
    ys = x_sample.reshape(db * tn, d)
    tm_p = min(256, seq)
    tm_s = db * tn
    lists = [[] for _ in range(7)]
    for i in range(depth):
        gains = norm_gains[i]
        w_fin = w_ffn_in[i].astype(BF16)
        w_fout = w_ffn_out[i].astype(BF16)
        if i % 2 == 0:
            a = i // 2
            mp, ms, caches = _nsa_layer(yp, ys, a, cache_cmp_kv, cache_sel_kv, cache_win_kv,
                                        page_table, gains, w_nsa_in, w_cmp_hidden, w_cmp_out,
                                        cmp_pos_emb, dims)
            for lst, c in zip(lists[:6], caches):
                lst.append(c)
            wo = w_nsa_out[a].astype(BF16)
        else:
            bi = i // 2
            w_in = w_gm_in[bi].astype(BF16)
            ln_g = gm_norm_gain[bi].reshape(1, -1)
            ws = w_spatial[bi]
            bs = b_spatial[bi]
            mp = _gmlp(yp, gains, w_in, ln_g, ws, bs.T, tm_p, False)[0]
            eye = jnp.eye(db, dtype=ws.dtype)
            ws_s = jnp.einsum('bc,gts->gbtcs', eye, ws[:, :tn, :tn]).reshape(-1, tm_s, tm_s)
            bs_s = jnp.tile(bs[:, :tn], (1, db)).T
            ms, v_new = _gmlp(ys, gains, w_in, ln_g, ws_s, bs_s, tm_s, True)
            lists[6].append(v_new.reshape(db, tn, -1))
            wo = w_gm_out[bi].astype(BF16)
        yp = _post(mp, yp, wo, gains, w_fin, w_fout, tm_p)
        ys = _post(ms, ys, wo, gains, w_fin, w_fout, tm_s)
    return (yp.reshape(bsz, seq, d), ys.reshape(db, tn, d)) + tuple(jnp.stack(l) for l in lists)
```

```python
import functools

import jax
import jax.numpy as jnp
from jax import lax
from jax.experimental import pallas as pl
from jax.experimental.pallas import tpu as pltpu

F32 = jnp.float32
BF16 = jnp.bfloat16

HEAD_DIM = 64
KV_GROUPS = 4
GROUP_SIZE = 4
N_HEADS = KV_GROUPS * GROUP_SIZE
Q_DIM = N_HEADS * HEAD_DIM
GKV = KV_GROUPS * HEAD_DIM
KV_DIM = 2 * GKV
CMP_STRIDE = 16
CMP_BLOCK = 2 * CMP_STRIDE
SEL_BLOCK = 64
SEL_TOPN = 16
N_LOCAL = 2
WINDOW = 512
PAGE_SIZE = 128
CHUNK = 128
GMLP_GROUPS = 8
ATTN_SCALE = HEAD_DIM ** -0.5
EPS = 1e-6
NEG = -1e30
FORCE = 1e4
MASK_BIG = 1e30
LANES = 128
VMEM_LIMIT = 56 * 1024 * 1024


def _params(sem):
    return pltpu.CompilerParams(dimension_semantics=sem, vmem_limit_bytes=VMEM_LIMIT)


def _resident(shape):
    nd = len(shape)
    return pl.BlockSpec(shape, lambda *_: (0,) * nd, pipeline_mode=pl.Buffered(1))


def _rms(x, g):
    return x * lax.rsqrt(jnp.mean(x * x, axis=-1, keepdims=True) + EPS) * g


def _dot(a, b):
    return jnp.dot(a, b, preferred_element_type=F32)


def _dot_nt(a, b):
    return lax.dot_general(a, b, (((1,), (1,)), ((), ())), preferred_element_type=F32)


def _post_kernel(m_ref, x_ref, wo_ref, g_ref, win_ref, wout_ref, y_ref, *, ffn, ck):
    x = x_ref[...]
    y1 = x + _rms(_dot(m_ref[...], wo_ref[...]), g_ref[1:2, :])
    h = _rms(y1, g_ref[2:3, :]).astype(BF16)
    acc = jnp.zeros_like(x)
    for c0 in range(0, ffn, ck):
        w = min(ck, ffn - c0)
        a = _dot(h, win_ref[:, c0:c0 + w])
        b = _dot(h, win_ref[:, ffn + c0:ffn + c0 + w])
        act = (jax.nn.silu(a) * b).astype(BF16)
        acc = acc + _dot(act, wout_ref[c0:c0 + w, :])
    y_ref[...] = y1 + _rms(acc, g_ref[3:4, :])


def _post(m, x, wo, gains, w_in, w_out, tm):
    rows, d = x.shape
    km = m.shape[1]
    ffn = w_out.shape[0]
    return pl.pallas_call(
        functools.partial(_post_kernel, ffn=ffn, ck=256),
        grid=(rows // tm,),
        in_specs=[
            pl.BlockSpec((tm, km), lambda i: (i, 0)),
            pl.BlockSpec((tm, d), lambda i: (i, 0)),
            _resident(wo.shape),
            _resident(gains.shape),
            _resident(w_in.shape),
            _resident(w_out.shape),
        ],
        out_specs=pl.BlockSpec((tm, d), lambda i: (i, 0)),
        out_shape=jax.ShapeDtypeStruct((rows, d), F32),
        compiler_params=_params(("parallel",)),
        name="post_mixer",
    )(m, x, wo, gains, w_in, w_out)


def _gmlp_kernel(x_ref, g_ref, win_ref, lng_ref, wm_ref, bs_ref, t_ref, *v_ref, nchunk, gd):
    h = _rms(x_ref[...], g_ref[0:1, :]).astype(BF16)
    u = jax.nn.gelu(_dot(h, win_ref[:, :gd]))
    vp = jax.nn.gelu(_dot(h, win_ref[:, gd:]))
    xc = vp - jnp.mean(vp, axis=-1, keepdims=True)
    v = xc * lax.rsqrt(jnp.mean(xc * xc, axis=-1, keepdims=True) + EPS) * lng_ref[...]
    if v_ref:
        v_ref[0][...] = v
    vb = v.astype(BF16)
    gw = gd // GMLP_GROUPS
    tri = (lax.broadcasted_iota(jnp.int32, (CHUNK, CHUNK), 0)
           >= lax.broadcasted_iota(jnp.int32, (CHUNK, CHUNK), 1))
    for gg in range(GMLP_GROUPS):
        wmg = jnp.where(tri, wm_ref[gg], 0.0).astype(BF16)
        bias = bs_ref[:, gg:gg + 1]
        for c in range(nchunk):
            rs = slice(c * CHUNK, (c + 1) * CHUNK)
            cs = slice(gg * gw, (gg + 1) * gw)
            mixed = _dot(wmg, vb[rs, cs]) + bias
            t_ref[rs, cs] = (u[rs, cs] * mixed).astype(BF16)


def _gmlp(x, gains, w_in, ln_g, wm, bs_t, tm, emit_v):
    rows, d = x.shape
    gd = w_in.shape[1] // 2
    out_shape = [jax.ShapeDtypeStruct((rows, gd), BF16)]
    out_specs = [pl.BlockSpec((tm, gd), lambda i: (i, 0))]
    if emit_v:
        out_shape.append(jax.ShapeDtypeStruct((rows, gd), F32))
        out_specs.append(pl.BlockSpec((tm, gd), lambda i: (i, 0)))
    return pl.pallas_call(
        functools.partial(_gmlp_kernel, nchunk=tm // CHUNK, gd=gd),
        grid=(rows // tm,),
        in_specs=[
            pl.BlockSpec((tm, d), lambda i: (i, 0)),
            _resident(gains.shape),
            _resident(w_in.shape),
            _resident(ln_g.shape),
            _resident(wm.shape),
            _resident(bs_t.shape),
        ],
        out_specs=out_specs,
        out_shape=out_shape,
        compiler_params=_params(("parallel",)),
        name="gmlp_mix",
    )(x, gains, w_in, ln_g, wm, bs_t)


def _nsa_proj_kernel(x_ref, g_ref, wq_ref, wkv_ref, wg_ref,
                     q_ref, kvc_ref, kvs_ref, kvw_ref, kvct_ref, kvst_ref, kvwt_ref, gate_ref,
                     ksel_ref, vselt_ref, kwin_ref, vwint_ref, *, tm, seq):
    h = _rms(x_ref[...], g_ref[0:1, :]).astype(BF16)
    q_ref[...] = (_dot(h, wq_ref[...]) * ATTN_SCALE).astype(BF16)
    kv = _dot(h, wkv_ref[...])
    kvc = kv[:, :KV_DIM]
    kvs = kv[:, KV_DIM:2 * KV_DIM]
    kvw = kv[:, 2 * KV_DIM:]
    kvc_ref[...] = kvc
    kvs_ref[...] = kvs
    kvw_ref[...] = kvw
    kvst = kvs.T
    kvwt = kvw.T
    kvct = kvc.T
    for p in range(tm // PAGE_SIZE):
        cols = slice(p * PAGE_SIZE, (p + 1) * PAGE_SIZE)
        kvct_ref[p] = kvct[:, cols]
        kvst_ref[p] = kvst[:, cols]
        kvwt_ref[p] = kvwt[:, cols]
    gz = _dot(h, wg_ref[...])
    for g in range(KV_GROUPS):
        gate_ref[g] = jax.nn.sigmoid(gz[:, g * LANES:(g + 1) * LANES])
    t0 = lax.rem(pl.program_id(0) * tm, seq)
    blk = (t0 + lax.broadcasted_iota(jnp.int32, (tm, LANES), 0)) // SEL_BLOCK
    onehot = jnp.where(lax.broadcasted_iota(jnp.int32, (tm, LANES), 1) == blk,
                       -MASK_BIG, 0.0).astype(BF16)
    ones = jnp.ones((HEAD_DIM, tm), BF16)
    for g in range(KV_GROUPS):
        ks = slice(g * HEAD_DIM, (g + 1) * HEAD_DIM)
        vs = slice(GKV + g * HEAD_DIM, GKV + (g + 1) * HEAD_DIM)
        ksel_ref[g, :, 0:LANES] = onehot
        ksel_ref[g, :, LANES:LANES + HEAD_DIM] = kvs[:, ks].astype(BF16)
        kwin_ref[g] = kvw[:, ks].astype(BF16)
        vselt_ref[g, 0:HEAD_DIM, :] = kvst[vs, :].astype(BF16)
        vselt_ref[g, HEAD_DIM:, :] = ones
        vwint_ref[g, 0:HEAD_DIM, :] = kvwt[vs, :].astype(BF16)
        vwint_ref[g, HEAD_DIM:, :] = ones


def _nsa_proj(x, gains, wq, wkv, wg, tm, seq):
    rows, d = x.shape
    row = lambda i: (i, 0)
    grow = lambda i: (0, i, 0)
    gcol = lambda i: (0, 0, i)
    kaug = LANES + HEAD_DIM
    npg = tm // PAGE_SIZE
    f32 = lambda *s: jax.ShapeDtypeStruct(s, F32)
    bf16 = lambda *s: jax.ShapeDtypeStruct(s, BF16)
    out_shape = [
        bf16(rows, Q_DIM),
        f32(rows, KV_DIM), f32(rows, KV_DIM), f32(rows, KV_DIM),
        f32(rows // PAGE_SIZE, KV_DIM, PAGE_SIZE), f32(rows // PAGE_SIZE, KV_DIM, PAGE_SIZE),
        f32(rows // PAGE_SIZE, KV_DIM, PAGE_SIZE),
        f32(KV_GROUPS, rows, LANES),
        bf16(KV_GROUPS, rows, kaug), bf16(KV_GROUPS, 2 * HEAD_DIM, rows),
        bf16(KV_GROUPS, rows, HEAD_DIM), bf16(KV_GROUPS, 2 * HEAD_DIM, rows),
    ]
    page = pl.BlockSpec((npg, KV_DIM, PAGE_SIZE), lambda i: (i, 0, 0))
    out_specs = [
        pl.BlockSpec((tm, Q_DIM), row),
        pl.BlockSpec((tm, KV_DIM), row), pl.BlockSpec((tm, KV_DIM), row),
        pl.BlockSpec((tm, KV_DIM), row),
        page, page, page,
        pl.BlockSpec((KV_GROUPS, tm, LANES), grow),
        pl.BlockSpec((KV_GROUPS, tm, kaug), grow),
        pl.BlockSpec((KV_GROUPS, 2 * HEAD_DIM, tm), gcol),
        pl.BlockSpec((KV_GROUPS, tm, HEAD_DIM), grow),
        pl.BlockSpec((KV_GROUPS, 2 * HEAD_DIM, tm), gcol),
    ]
    return pl.pallas_call(
        functools.partial(_nsa_proj_kernel, tm=tm, seq=seq),
        grid=(rows // tm,),
        in_specs=[
            pl.BlockSpec((tm, d), row),
            _resident(gains.shape),
            _resident(wq.shape),
            _resident(wkv.shape),
            _resident(wg.shape),
        ],
        out_specs=out_specs,
        out_shape=out_shape,
        compiler_params=_params(("parallel",)),
        name="nsa_proj",
    )(x, gains, wq, wkv, wg)


def _compress_ab_compute(x_refs, nrow, bd_ref, ab_ref):
    per_kv = GKV // LANES
    for kv in range(2):
        acc = None
        for l in range(CMP_STRIDE):
            xl = jnp.concatenate(
                [x_refs[kv * per_kv + h][pl.ds(l, nrow, stride=CMP_STRIDE), :]
                 for h in range(per_kv)], axis=1).astype(BF16)
            part = _dot(xl, bd_ref[kv, l])
            acc = part if acc is None else acc + part
        ab_ref[:, kv * 2 * GKV:(kv + 1) * 2 * GKV] = acc


def _compress_ab_kernel(*refs, nrow):
    x_refs, (bd_ref, ab_ref) = refs[:-2], refs[-2:]
    _compress_ab_compute(x_refs, nrow, bd_ref, ab_ref)


def _compress_ab(x, bd, nb):
    seq = x.shape[0] // nb
    n = seq // CMP_STRIDE
    rb = min(n, 256)
    nr = n // rb
    ncol = KV_DIM // LANES
    xspecs = [pl.BlockSpec((rb * CMP_STRIDE, LANES), lambda b, r, j=j: (b * nr + r, j))
              for j in range(ncol)]
    return pl.pallas_call(
        functools.partial(_compress_ab_kernel, nrow=rb),
        grid=(nb, nr),
        in_specs=xspecs + [_resident(bd.shape)],
        out_specs=pl.BlockSpec((None, rb, 4 * GKV), lambda b, r: (b, r, 0)),
        out_shape=jax.ShapeDtypeStruct((nb, n, 4 * GKV), F32),
        compiler_params=_params(("parallel", "parallel")),
        name="compress_ab",
    )(*([x] * ncol), bd)


def _compress_ab_paged_kernel(pt_ref, pool_ref, bd_ref, ab_ref, xbuf, *rest, pgs, nsplit):
    xrows, sem = rest[:-1], rest[-1]
    s = pl.program_id(0)
    nsteps = pl.num_programs(0)

    def copies(step, slot):
        b = step // nsplit
        h = step % nsplit
        return [pltpu.make_async_copy(pool_ref.at[pt_ref[b, h * pgs + p]], xbuf.at[slot, p],
                                      sem.at[slot]) for p in range(pgs)]

    @pl.when(s == 0)
    def _():
        for c in copies(s, 0):
            c.start()

    @pl.when(s + 1 < nsteps)
    def _():
        for c in copies(s + 1, (s + 1) % 2):
            c.start()

    slot = s % 2
    for c in copies(s, slot):
        c.wait()

    def body(p, carry):
        r0 = pl.multiple_of(p * PAGE_SIZE, PAGE_SIZE)
        per_kv = GKV // LANES
        for kv in range(2):
            xt = xbuf[slot, p, kv].T
            for h in range(per_kv):
                xrows[kv * per_kv + h][pl.ds(r0, PAGE_SIZE), :] = xt[:, h * LANES:(h + 1) * LANES]
        return carry

    lax.fori_loop(0, pgs, body, 0)
    _compress_ab_compute(xrows, pgs * PAGE_SIZE // CMP_STRIDE, bd_ref, ab_ref)


def _compress_ab_paged(pt, pool_t, bd, pgs):
    nb, n_pages = pt.shape
    nsplit = n_pages // pgs
    rows_pp = PAGE_SIZE // CMP_STRIDE
    rb = pgs * rows_pp
    grid_spec = pltpu.PrefetchScalarGridSpec(
        num_scalar_prefetch=1,
        grid=(nb * nsplit,),
        in_specs=[pl.BlockSpec(memory_space=pl.ANY),
                  pl.BlockSpec(bd.shape, lambda s, pt: (0,) * 4, pipeline_mode=pl.Buffered(1))],
        out_specs=pl.BlockSpec((None, rb, 4 * GKV), lambda s, pt: (s // nsplit, s % nsplit, 0)),
        scratch_shapes=([pltpu.VMEM((2, pgs, 2, GKV, PAGE_SIZE), F32)]
                        + [pltpu.VMEM((pgs * PAGE_SIZE, LANES), F32)] * (KV_DIM // LANES)
                        + [pltpu.SemaphoreType.DMA((2,))]),
    )
    return pl.pallas_call(
        functools.partial(_compress_ab_paged_kernel, pgs=pgs, nsplit=nsplit),
        grid_spec=grid_spec,
        out_shape=jax.ShapeDtypeStruct((nb, n_pages * rows_pp, 4 * GKV), F32),
        compiler_params=_params(("arbitrary",)),
        name="compress_ab_paged",
    )(pt, pool_t, bd)


def _compress_fin_kernel(ab_ref, pe_ref, w1_ref, w2_ref, kc_ref, vct_ref, *, n):
    for kv in range(2):
        a = ab_ref[:, kv * 2 * GKV:kv * 2 * GKV + GKV]
        b = ab_ref[:, kv * 2 * GKV + GKV:(kv + 1) * 2 * GKV]
        c = _dot(pe_ref[kv], w1_ref[kv])[0:1, :]
        c4 = jnp.concatenate([c] * KV_GROUPS, axis=1)
        hid = a + pltpu.roll(b, n - 1, 0) + c4
        o = _dot(jax.nn.gelu(hid).astype(BF16), w2_ref[kv])
        if kv == 0:
            for g in range(KV_GROUPS):
                kc_ref[g] = o[:, g * HEAD_DIM:(g + 1) * HEAD_DIM].astype(BF16)
        else:
            ot = o.T
            for g in range(KV_GROUPS):
                vct_ref[g] = ot[g * HEAD_DIM:(g + 1) * HEAD_DIM, :].astype(BF16)


def _compress_fin(ab, pe8, w1f, w2bd):
    nb, n, _ = ab.shape
    return pl.pallas_call(
        functools.partial(_compress_fin_kernel, n=n),
        grid=(nb,),
        in_specs=[pl.BlockSpec((None, n, 4 * GKV), lambda b: (b, 0, 0)),
                  _resident(pe8.shape), _resident(w1f.shape), _resident(w2bd.shape)],
        out_specs=[pl.BlockSpec((None, KV_GROUPS, n, HEAD_DIM), lambda b: (b, 0, 0, 0)),
                   pl.BlockSpec((None, KV_GROUPS, HEAD_DIM, n), lambda b: (b, 0, 0, 0))],
        out_shape=[jax.ShapeDtypeStruct((nb, KV_GROUPS, n, HEAD_DIM), BF16),
                   jax.ShapeDtypeStruct((nb, KV_GROUPS, HEAD_DIM, n), BF16)],
        compiler_params=_params(("parallel",)),
        name="compress_fin",
    )(ab, pe8, w1f, w2bd)


def _group_rows(qt):
    return jnp.concatenate([qt[:, r * HEAD_DIM:(r + 1) * HEAD_DIM] for r in range(GROUP_SIZE)],
                           axis=0)


def _ungroup_t(ot, tq):
    return jnp.concatenate([ot[:, r * tq:(r + 1) * tq].T for r in range(GROUP_SIZE)], axis=1)


def _lane_qpos(s0, tq):
    q = lax.broadcasted_iota(jnp.int32, (1, tq), 1)
    return s0 + jnp.concatenate([q] * GROUP_SIZE, axis=1)


def _gate_lanes(gtt, j, tq):
    return jnp.concatenate([gtt[3 * r + j:3 * r + j + 1, :] for r in range(GROUP_SIZE)], axis=1)


def _topk_mask(score, k):
    nb = score.shape[0]
    idx = lax.broadcasted_iota(jnp.int32, score.shape, 0).astype(F32)

    def body(_, carry):
        work, sel = carry
        m = jnp.max(work, axis=0, keepdims=True)
        first = jnp.min(jnp.where(work == m, idx, float(nb)), axis=0, keepdims=True)
        pick = idx == first
        return jnp.where(pick, -3e38, work), jnp.where(pick, 1.0, sel)

    _, sel = lax.fori_loop(0, k, body, (score, jnp.zeros_like(score)), unroll=True)
    return sel


def _cmp_select_kernel(q_ref, kc_ref, vct_ref, ovt_ref, gate_ref, oc_ref, nsel_ref,
                       *, tq, pos0, k_top):
    s0 = pos0 + pl.program_id(2) * tq
    qrows = _group_rows(q_ref[...])
    n = kc_ref.shape[0]
    st = _dot_nt(kc_ref[...], qrows)
    qpos = _lane_qpos(s0, tq)
    end = lax.broadcasted_iota(jnp.int32, (n, 1), 0) * CMP_STRIDE + (CMP_BLOCK - 1)
    sm = jnp.where(end <= qpos, st, NEG)
    e = jnp.exp(sm - jnp.max(sm, axis=0, keepdims=True))
    p = e * (1.0 / jnp.sum(e, axis=0, keepdims=True))
    p = jnp.where(qpos >= CMP_BLOCK - 1, p, 0.0)
    oct_ = _dot(vct_ref[...], p.astype(BF16))
    gtt = gate_ref[...].T
    oc_ref[...] = _ungroup_t(_gate_lanes(gtt, 0, tq) * oct_, tq).astype(BF16)
    psum = p[:, 0:tq]
    for r in range(1, GROUP_SIZE):
        psum = psum + p[:, r * tq:(r + 1) * tq]
    hi = psum.astype(BF16)
    lo = (psum - hi.astype(F32)).astype(BF16)
    imp_t = _dot(ovt_ref[...], hi) + _dot(ovt_ref[...], lo)
    shape = imp_t.shape
    blk = lax.broadcasted_iota(jnp.int32, shape, 0)
    cur = (s0 + lax.broadcasted_iota(jnp.int32, shape, 1)) // SEL_BLOCK
    valid = blk <= cur
    forced = (blk == 0) | (blk > cur - N_LOCAL)
    score = jnp.where(valid, jnp.where(forced, FORCE, imp_t), NEG)
    sel = _topk_mask(score, k_top)
    nsel_t = jnp.where((sel > 0.5) & valid, 0.0, 1.0)
    nsel_ref[...] = nsel_t.T.astype(BF16)


def _cmp_select(q, kc, vct, ovt, gates, tpad, pos0, k_top):
    nb, _, n, _ = kc.shape
    tq = 128
    nq = tpad // tq
    nbp = ovt.shape[0]
    gq = GROUP_SIZE * HEAD_DIM
    return pl.pallas_call(
        functools.partial(_cmp_select_kernel, tq=tq, pos0=pos0, k_top=k_top),
        grid=(nb, KV_GROUPS, nq),
        in_specs=[
            pl.BlockSpec((tq, gq), lambda b, g, i: (b * nq + i, g)),
            pl.BlockSpec((None, None, n, HEAD_DIM), lambda b, g, i: (b, g, 0, 0)),
            pl.BlockSpec((None, None, HEAD_DIM, n), lambda b, g, i: (b, g, 0, 0)),
            pl.BlockSpec(ovt.shape, lambda b, g, i: (0, 0)),
            pl.BlockSpec((None, tq, LANES), lambda b, g, i: (g, b * nq + i, 0)),
        ],
        out_specs=[
            pl.BlockSpec((tq, gq), lambda b, g, i: (b * nq + i, g)),
            pl.BlockSpec((None, None, tq, nbp), lambda b, g, i: (b, g, i, 0)),
        ],
        out_shape=[
            jax.ShapeDtypeStruct((nb * tpad, Q_DIM), BF16),
            jax.ShapeDtypeStruct((nb, KV_GROUPS, tpad, nbp), BF16),
        ],
        compiler_params=_params(("parallel", "parallel", "parallel")),
        name="cmp_select",
    )(q, kc, vct, ovt, gates)


def _sel_win_kernel(q_ref, nsel_ref, gate_ref, oc_ref, ksel_ref, vselt_ref, kwin_ref, vwint_ref,
                    o_ref, *, tq, tk, seq):
    s0 = pl.program_id(2) * tq
    cols = GROUP_SIZE * tq
    qrows = _group_rows(q_ref[...])
    ns = nsel_ref[...]
    qaug = jnp.concatenate([jnp.concatenate([ns] * GROUP_SIZE, axis=0), qrows], axis=1)
    qpos = _lane_qpos(s0, tq)
    gtt = gate_ref[...].T

    def scores(j):
        k0 = pl.multiple_of(j * tk, tk)
        return _dot_nt(ksel_ref[pl.ds(k0, tk), :], qaug)

    def update(st, j, m, acc):
        k0 = pl.multiple_of(j * tk, tk)
        mn = jnp.maximum(m, jnp.max(st, axis=0, keepdims=True))
        p = jnp.exp(st - mn).astype(BF16)
        return mn, jnp.exp(m - mn) * acc + _dot(vselt_ref[:, pl.ds(k0, tk)], p)

    jd = s0 // tk

    def body(j, carry):
        st, m, acc = carry
        st_next = scores(j + 1)
        m, acc = update(st, j, m, acc)
        return st_next, m, acc

    carry = (scores(0), jnp.full((1, cols), NEG, F32), jnp.zeros((2 * HEAD_DIM, cols), F32))
    st, m, acc = lax.fori_loop(0, jd, body, carry)
    kpos = jd * tk + lax.broadcasted_iota(jnp.int32, (tk, 1), 0)
    _, acc = update(jnp.where(kpos <= qpos, st, NEG), jd, m, acc)
    o_t = _gate_lanes(gtt, 1, tq) * (acc[:HEAD_DIM] / acc[HEAD_DIM:HEAD_DIM + 1])

    nw = min(WINDOW + tq, seq)
    w0 = pl.multiple_of(jnp.maximum(s0 + tq - nw, 0), tq)
    sw = _dot_nt(kwin_ref[pl.ds(w0, nw), :], qrows)
    d = qpos - (w0 + lax.broadcasted_iota(jnp.int32, (nw, 1), 0))
    sw = jnp.where((d >= 0) & (d < WINDOW), sw, NEG)
    pw = jnp.exp(sw - jnp.max(sw, axis=0, keepdims=True)).astype(BF16)
    accw = _dot(vwint_ref[:, pl.ds(w0, nw)], pw)
    o_t = o_t + _gate_lanes(gtt, 2, tq) * (accw[:HEAD_DIM] / accw[HEAD_DIM:HEAD_DIM + 1])
    o_ref[...] = (oc_ref[...].astype(F32) + _ungroup_t(o_t, tq)).astype(BF16)


def _sel_win(q, nsel, gates, ocg, ksel, vselt, kwin, vwint, nb, seq):
    tq = 128
    tk = min(512, seq)
    nq = seq // tq
    gq = GROUP_SIZE * HEAD_DIM
    qspec = pl.BlockSpec((tq, gq), lambda b, g, i: (b * nq + i, g))
    krows = lambda w: pl.BlockSpec((None, seq, w), lambda b, g, i: (g, b, 0))
    vcols = pl.BlockSpec((None, 2 * HEAD_DIM, seq), lambda b, g, i: (g, 0, b))
    return pl.pallas_call(
        functools.partial(_sel_win_kernel, tq=tq, tk=tk, seq=seq),
        grid=(nb, KV_GROUPS, nq),
        in_specs=[
            qspec,
            pl.BlockSpec((None, None, tq, LANES), lambda b, g, i: (b, g, i, 0)),
            pl.BlockSpec((None, tq, LANES), lambda b, g, i: (g, b * nq + i, 0)),
            qspec,
            krows(LANES + HEAD_DIM), vcols, krows(HEAD_DIM), vcols,
        ],
        out_specs=qspec,
        out_shape=jax.ShapeDtypeStruct((nb * seq, Q_DIM), BF16),
        compiler_params=_params(("parallel", "parallel", "arbitrary")),
        name="sel_win_attn",
    )(q, nsel, gates, ocg, ksel, vselt, kwin, vwint)


def _sample_attn_kernel(pt_ref, pool_ref, qbd_ref, nselr_ref, eneg_ref, knew_ref, wnew_ref,
                        wbuf_ref, gate_ref, ocg_ref, o_ref,
                        kbuf, sem, m_sc, l_sc, acc_sc, *, pgs, nch, ppt, past, tn):
    b = pl.program_id(0)
    c = pl.program_id(1)
    step = b * nch + c
    nsteps = pl.num_programs(0) * nch

    def copies(st, slot):
        bb = st // nch
        cc = st % nch
        return [pltpu.make_async_copy(pool_ref.at[pt_ref[bb, cc * pgs + p]], kbuf.at[slot, p],
                                      sem.at[slot]) for p in range(pgs)]

    @pl.when(step == 0)
    def _():
        for cp in copies(step, 0):
            cp.start()

    @pl.when(step + 1 < nsteps)
    def _():
        for cp in copies(step + 1, (step + 1) % 2):
            cp.start()

    @pl.when(c == 0)
    def _():
        m_sc[...] = jnp.full(m_sc.shape, NEG, F32)
        l_sc[...] = jnp.zeros(l_sc.shape, F32)
        acc_sc[...] = jnp.zeros(acc_sc.shape, F32)

    slot = step % 2
    for cp in copies(step, slot):
        cp.wait()

    qbd = qbd_ref[...]
    nselr = nselr_ref[...]
    rows = qbd.shape[0]
    qpos = past + lax.rem(lax.broadcasted_iota(jnp.int32, (rows, 1), 0), tn)

    def online(s, pv):
        m = m_sc[...]
        mn = jnp.maximum(m, jnp.max(s, axis=-1, keepdims=True))
        p = jnp.exp(s - mn)
        alpha = jnp.exp(m - mn)
        l_sc[...] = alpha * l_sc[...] + jnp.sum(p, axis=-1, keepdims=True)
        acc_sc[...] = alpha * acc_sc[...] + pv(p.astype(BF16))
        m_sc[...] = mn

    tk = ppt * PAGE_SIZE
    for t in range(pgs // ppt):
        kt = jnp.concatenate([kbuf[slot, t * ppt + i, 0] for i in range(ppt)], axis=1).astype(BF16)
        vt = jnp.concatenate([kbuf[slot, t * ppt + i, 1] for i in range(ppt)], axis=1).astype(BF16)
        k0 = pl.multiple_of(c * (pgs * PAGE_SIZE) + t * tk, tk)
        bias = _dot_nt(nselr, eneg_ref[pl.ds(k0, tk), :])
        online(_dot(qbd, kt) + bias, lambda p, vt=vt: _dot_nt(p, vt))

    @pl.when(c == nch - 1)
    def _():
        npad = knew_ref.shape[0]
        newpos = past + lax.broadcasted_iota(jnp.int32, (1, npad), 1)
        new_ok = (newpos <= qpos) & (newpos < past + tn)
        kn = knew_ref[:, 0:GKV].astype(BF16)
        vn = knew_ref[:, GKV:].astype(BF16)
        online(jnp.where(new_ok, _dot_nt(qbd, kn), NEG), lambda p: _dot(p, vn))
        o_s = acc_sc[...] / l_sc[...]
        wbl = wbuf_ref.shape[2]
        dw = qpos - (past - wbl + lax.broadcasted_iota(jnp.int32, (1, wbl), 1))
        s1 = jnp.where((dw >= 0) & (dw < WINDOW), _dot(qbd, wbuf_ref[0].astype(BF16)), NEG)
        s2 = jnp.where(new_ok & (qpos - newpos < WINDOW),
                       _dot_nt(qbd, wnew_ref[:, 0:GKV].astype(BF16)), NEG)
        mw = jnp.maximum(jnp.max(s1, axis=-1, keepdims=True), jnp.max(s2, axis=-1, keepdims=True))
        p1 = jnp.exp(s1 - mw)
        p2 = jnp.exp(s2 - mw)
        lw = jnp.sum(p1, axis=-1, keepdims=True) + jnp.sum(p2, axis=-1, keepdims=True)
        o_w = (_dot_nt(p1.astype(BF16), wbuf_ref[1].astype(BF16))
               + _dot(p2.astype(BF16), wnew_ref[:, GKV:].astype(BF16))) / lw
        gt = gate_ref[...]
        o_ref[...] = ocg_ref[...] + gt[:, 1:2] * o_s + gt[:, 2:3] * o_w


def _sample_attn(pt, pool_t, qbd, nselr, eneg, knew, wnew, wbuf_t, wbuf_off, gates_r, ocg_r,
                 pgs, past, tn):
    nb, n_pages = pt.shape
    nch = n_pages // pgs
    rows = qbd.shape[1]
    ppt = min(4, pgs)
    per_b = lambda shape: pl.BlockSpec((None,) + shape, lambda b, c, pt: (b,) + (0,) * len(shape))
    grid_spec = pltpu.PrefetchScalarGridSpec(
        num_scalar_prefetch=1,
        grid=(nb, nch),
        in_specs=[
            pl.BlockSpec(memory_space=pl.ANY),
            per_b(qbd.shape[1:]),
            per_b(nselr.shape[1:]),
            pl.BlockSpec(eneg.shape, lambda b, c, pt: (0, 0), pipeline_mode=pl.Buffered(1)),
            per_b(knew.shape[1:]),
            per_b(wnew.shape[1:]),
            pl.BlockSpec((None,) + wbuf_t.shape[1:], lambda b, c, pt: (wbuf_off + b, 0, 0, 0)),
            per_b(gates_r.shape[1:]),
            per_b(ocg_r.shape[1:]),
        ],
        out_specs=per_b((rows, GKV)),
        scratch_shapes=[
            pltpu.VMEM((2, pgs, 2, GKV, PAGE_SIZE), F32),
            pltpu.SemaphoreType.DMA((2,)),
            pltpu.VMEM((rows, 1), F32),
            pltpu.VMEM((rows, 1), F32),
            pltpu.VMEM((rows, GKV), F32),
        ],
    )
    return pl.pallas_call(
        functools.partial(_sample_attn_kernel, pgs=pgs, nch=nch, ppt=ppt, past=past, tn=tn),
        grid_spec=grid_spec,
        out_shape=jax.ShapeDtypeStruct((nb, rows, GKV), F32),
        compiler_params=_params(("arbitrary", "arbitrary")),
        name="sample_attn",
    )(pt, pool_t, qbd, nselr, eneg, knew, wnew, wbuf_t, gates_r, ocg_r)


def _overlap_t(nbp, nrow):
    cs = jnp.arange(nrow)[None, :] * CMP_STRIDE
    ss = jnp.arange(nbp)[:, None] * SEL_BLOCK
    ov = jnp.minimum(cs + CMP_BLOCK, ss + SEL_BLOCK) - jnp.maximum(cs, ss)
    return (jnp.clip(ov, 0).astype(F32) / CMP_BLOCK).astype(BF16)


def _block_diag_w1(w1):
    eye = jnp.eye(KV_GROUPS, dtype=w1.dtype)
    bd = jnp.einsum('gh,klde->klgdhe', eye, w1).reshape(2, CMP_BLOCK, GKV, GKV)
    return jnp.concatenate([bd[:, :CMP_STRIDE], bd[:, CMP_STRIDE:]], axis=-1).astype(BF16)


def _block_diag_w2(w2):
    eye = jnp.eye(KV_GROUPS, dtype=w2.dtype)
    return jnp.einsum('gh,kde->kgdhe', eye, w2).reshape(2, GKV, GKV).astype(BF16)


def _gate_weight(wg):
    d = wg.shape[0]
    w = wg.reshape(d, KV_GROUPS, 3 * GROUP_SIZE)
    w = jnp.pad(w, ((0, 0), (0, 0), (0, LANES - 3 * GROUP_SIZE)))
    return w.reshape(d, KV_GROUPS * LANES).astype(BF16)


def _token_minor(cache):
    nd = cache.ndim
    perm = tuple(range(nd - 4)) + (nd - 3, nd - 2, nd - 1, nd - 4)
    t = cache.transpose(perm)
    return t.reshape(t.shape[:nd - 3] + (GKV, t.shape[-1]))


def _from_pages(pages_t, lead):
    x = pages_t.reshape(lead + (2, KV_GROUPS, HEAD_DIM, PAGE_SIZE))
    nd = x.ndim
    return x.transpose(tuple(range(nd - 4)) + (nd - 1, nd - 4, nd - 3, nd - 2))


def _nsa_layer(yp, ys, a, cache_cmp_kv, cache_sel_kv, cache_win_kv, page_table, gains,
               w_nsa_in, w_cmp_hidden, w_cmp_out, cmp_pos_emb, dims):
    bsz, seq, db, tn, past = dims
    n_pool = cache_cmp_kv.shape[1]
    n_pages = page_table.shape[1]
    w_in = w_nsa_in[a]
    wq = w_in[:, :Q_DIM].astype(BF16)
    wkv = w_in[:, Q_DIM:Q_DIM + 3 * KV_DIM].astype(BF16)
    wg = _gate_weight(w_in[:, Q_DIM + 3 * KV_DIM:])
    bd1 = _block_diag_w1(w_cmp_hidden[a])
    w2bd = _block_diag_w2(w_cmp_out[a])
    w1f = w_cmp_hidden[a].reshape(2, CMP_BLOCK * HEAD_DIM, HEAD_DIM).astype(BF16)
    pe8 = jnp.broadcast_to(cmp_pos_emb[a].reshape(2, 1, CMP_BLOCK * HEAD_DIM),
                           (2, 8, CMP_BLOCK * HEAD_DIM)).astype(BF16)

    tm = min(512, seq)
    (qp, kvc_p, _, _, kvct_p, kvst_p, kvwt_p, gate_p, ksel, vselt, kwin, vwint) = _nsa_proj(
        yp, gains, wq, wkv, wg, tm, seq)
    nrow = seq // CMP_STRIDE
    kc, vct = _compress_fin(_compress_ab(kvc_p, bd1, bsz), pe8, w1f, w2bd)
    ns_p = -(-seq // SEL_BLOCK)
    assert ns_p <= LANES
    ocg, nsel = _cmp_select(qp, kc, vct, _overlap_t(LANES, nrow), gate_p, seq, 0,
                            min(SEL_TOPN, ns_p))
    op = _sel_win(qp, nsel, gate_p, ocg, ksel, vselt, kwin, vwint, bsz, seq)

    rows_s = db * tn
    qs, kvc_s, kvs_s, kvw_s, _, _, _, gate_s, _, _, _, _ = _nsa_proj(
        ys, gains, wq, wkv, wg, rows_s, rows_s)
    pt_abs = page_table + a * n_pool
    pool_c = _token_minor(cache_cmp_kv).reshape(-1, 2, GKV, PAGE_SIZE)
    pool_s = _token_minor(cache_sel_kv).reshape(-1, 2, GKV, PAGE_SIZE)
    ab_s = _compress_ab_paged(pt_abs, pool_c, bd1, min(32, n_pages))
    kc_s, vct_s = _compress_fin(ab_s, pe8, w1f, w2bd)
    nrow_s = past // CMP_STRIDE
    ns_s = -(-(past + tn) // SEL_BLOCK)
    nbp_s = -(-ns_s // LANES) * LANES
    tpad = 128
    qs_pad = jnp.pad(qs.reshape(db, tn, Q_DIM), ((0, 0), (0, tpad - tn), (0, 0)))
    gate_pad = jnp.pad(gate_s.reshape(KV_GROUPS, db, tn, LANES),
                       ((0, 0), (0, 0), (0, tpad - tn), (0, 0)))
    ocg_s, nsel_s = _cmp_select(qs_pad.reshape(db * tpad, Q_DIM), kc_s, vct_s,
                                _overlap_t(nbp_s, nrow_s),
                                gate_pad.reshape(KV_GROUPS, db * tpad, LANES), tpad, past,
                                min(SEL_TOPN, ns_s))
    eye = jnp.eye(KV_GROUPS, dtype=BF16)
    q5 = qs.reshape(db, tn, KV_GROUPS, GROUP_SIZE, HEAD_DIM)
    rows = KV_GROUPS * GROUP_SIZE * tn
    qbd = jnp.einsum('bqgrd,gh->bgrqhd', q5, eye).reshape(db, rows, GKV)
    nselr = jnp.broadcast_to(nsel_s[:, :, None, :tn, :LANES],
                             (db, KV_GROUPS, GROUP_SIZE, tn, LANES)).reshape(db, rows, LANES)
    g4 = gate_s.reshape(KV_GROUPS, db, tn, LANES)[..., :3 * GROUP_SIZE]
    g4 = g4.reshape(KV_GROUPS, db, tn, GROUP_SIZE, 3).transpose(1, 0, 3, 2, 4)
    gates_r = jnp.pad(g4.reshape(db, rows, 3), ((0, 0), (0, 0), (0, LANES - 3)))
    oc5 = ocg_s.reshape(db, tpad, KV_GROUPS, GROUP_SIZE, HEAD_DIM)[:, :tn].astype(F32)
    ocg_r = jnp.einsum('bqgrd,gh->bgrqhd', oc5, jnp.eye(KV_GROUPS, dtype=F32)).reshape(
        db, rows, GKV)
    kpos = jnp.arange(past)[:, None] // SEL_BLOCK
    eneg = jnp.where(kpos == jnp.arange(LANES)[None, :], -MASK_BIG, 0.0).astype(BF16)
    npad = 128
    knew = jnp.pad(kvs_s.reshape(db, tn, KV_DIM), ((0, 0), (0, npad - tn), (0, 0)))
    wnew = jnp.pad(kvw_s.reshape(db, tn, KV_DIM), ((0, 0), (0, npad - tn), (0, 0)))
    wbl = cache_win_kv.shape[2]
    assert past >= wbl
    wbuf_t = _token_minor(cache_win_kv).reshape(-1, 2, GKV, wbl)
    o_rows = _sample_attn(pt_abs, pool_s, qbd, nselr, eneg, knew, wnew, wbuf_t, a * db,
                          gates_r, ocg_r, min(16, n_pages), past, tn)
    o6 = o_rows.reshape(db, KV_GROUPS, GROUP_SIZE, tn, KV_GROUPS, HEAD_DIM)
    o_s = jnp.einsum('bgrqhd,gh->bqgrd', o6, jnp.eye(KV_GROUPS, dtype=F32))
    o_s = o_s.reshape(rows_s, Q_DIM).astype(BF16)

    shp = (KV_GROUPS, HEAD_DIM)
    npg = seq // PAGE_SIZE
    wpg = min(WINDOW, seq) // PAGE_SIZE
    p_win = _from_pages(kvwt_p.reshape(bsz, npg, KV_DIM, PAGE_SIZE)[:, npg - wpg:], (bsz, wpg))
    kvw_s5 = kvw_s.reshape((db, tn, 2) + shp)
    win_new = jnp.concatenate([cache_win_kv[a], kvw_s5], axis=1)[:, tn:]
    caches = (_from_pages(kvct_p, (bsz, npg)), _from_pages(kvst_p, (bsz, npg)),
              p_win.reshape((bsz, wpg * PAGE_SIZE, 2) + shp),
              kvc_s.reshape((db, tn, 2) + shp), kvs_s.reshape((db, tn, 2) + shp), win_new)
    return op, o_s, caches


def kernel(x_prompt, x_sample, cache_cmp_kv, cache_sel_kv, cache_win_kv, page_table, norm_gains,
           w_nsa_in, w_cmp_hidden, w_cmp_out, cmp_pos_emb, w_nsa_out, w_gm_in, gm_norm_gain,
           w_spatial, b_spatial, w_gm_out, w_ffn_in, w_ffn_out):
    bsz, seq, d = x_prompt.shape
    db, tn, _ = x_sample.shape
    past = page_table.shape[1] * PAGE_SIZE
    depth = norm_gains.shape[0]
    assert seq % CHUNK == 0 and past % SEL_BLOCK == 0 and tn < CMP_STRIDE
    assert (db * tn) % 8 == 0 and CHUNK % tn == 0 and db * tn == CHUNK
    dims = (bsz, seq, db, tn, past)
    yp = x_prompt.reshape(bsz * seq, d)
    ys = x_sample.reshape(db * tn, d)
    tm_p = min(256, seq)
    tm_s = db * tn
    lists = [[] for _ in range(7)]
    for i in range(depth):
        gains = norm_gains[i]
        w_fin = w_ffn_in[i].astype(BF16)
        w_fout = w_ffn_out[i].astype(BF16)
        if i % 2 == 0:
            a = i // 2
            mp, ms, caches = _nsa_layer(yp, ys, a, cache_cmp_kv, cache_sel_kv, cache_win_kv,
                                        page_table, gains, w_nsa_in, w_cmp_hidden, w_cmp_out,
                                        cmp_pos_emb, dims)
            for lst, c in zip(lists[:6], caches):
                lst.append(c)
            wo = w_nsa_out[a].astype(BF16)
        else:
            bi = i // 2
            w_in = w_gm_in[bi].astype(BF16)
            ln_g = gm_norm_gain[bi].reshape(1, -1)
            ws = w_spatial[bi]
            bs = b_spatial[bi]
            mp = _gmlp(yp, gains, w_in, ln_g, ws, bs.T, tm_p, False)[0]
            eye = jnp.eye(db, dtype=ws.dtype)
            ws_s = jnp.einsum('bc,gts->gbtcs', eye, ws[:, :tn, :tn]).reshape(-1, tm_s, tm_s)
            bs_s = jnp.tile(bs[:, :tn], (1, db)).T
            ms, v_new = _gmlp(ys, gains, w_in, ln_g, ws_s, bs_s, tm_s, True)
            lists[6].append(v_new.reshape(db, tn, -1))
            wo = w_gm_out[bi].astype(BF16)
        yp = _post(mp, yp, wo, gains, w_fin, w_fout, tm_p)
        ys = _post(ms, ys, wo, gains, w_fin, w_fout, tm_s)
    return (yp.reshape(bsz, seq, d), ys.reshape(db, tn, d)) + tuple(jnp.stack(l) for l in lists)
```

```python
import functools

import jax
import jax.numpy as jnp
from jax import lax
from jax.experimental import pallas as pl
from jax.experimental.pallas import tpu as pltpu

F32 = jnp.float32
BF16 = jnp.bfloat16

HEAD_DIM = 64
KV_GROUPS = 4
GROUP_SIZE = 4
N_HEADS = KV_GROUPS * GROUP_SIZE
Q_DIM = N_HEADS * HEAD_DIM
GKV = KV_GROUPS * HEAD_DIM
KV_DIM = 2 * GKV
CMP_STRIDE = 16
CMP_BLOCK = 2 * CMP_STRIDE
SEL_BLOCK = 64
SEL_TOPN = 16
N_LOCAL = 2
WINDOW = 512
PAGE_SIZE = 128
CHUNK = 128
GMLP_GROUPS = 8
ATTN_SCALE = HEAD_DIM ** -0.5
EPS = 1e-6
NEG = -1e30
FORCE = 1e4
MASK_BIG = 1e30
LANES = 128
VMEM_LIMIT = 56 * 1024 * 1024


def _params(sem):
    return pltpu.CompilerParams(dimension_semantics=sem, vmem_limit_bytes=VMEM_LIMIT)


def _resident(shape):
    nd = len(shape)
    return pl.BlockSpec(shape, lambda *_: (0,) * nd, pipeline_mode=pl.Buffered(1))


def _rms(x, g):
    return x * lax.rsqrt(jnp.mean(x * x, axis=-1, keepdims=True) + EPS) * g


def _dot(a, b):
    return jnp.dot(a, b, preferred_element_type=F32)


def _dot_nt(a, b):
    return lax.dot_general(a, b, (((1,), (1,)), ((), ())), preferred_element_type=F32)


def _post_kernel(m_ref, x_ref, wo_ref, g_ref, win_ref, wout_ref, y_ref, *, ffn, ck):
    x = x_ref[...]
    y1 = x + _rms(_dot(m_ref[...], wo_ref[...]), g_ref[1:2, :])
    h = _rms(y1, g_ref[2:3, :]).astype(BF16)
    acc = jnp.zeros_like(x)
    for c0 in range(0, ffn, ck):
        w = min(ck, ffn - c0)
        a = _dot(h, win_ref[:, c0:c0 + w])
        b = _dot(h, win_ref[:, ffn + c0:ffn + c0 + w])
        act = (jax.nn.silu(a) * b).astype(BF16)
        acc = acc + _dot(act, wout_ref[c0:c0 + w, :])
    y_ref[...] = y1 + _rms(acc, g_ref[3:4, :])


def _post(m, x, wo, gains, w_in, w_out, tm):
    rows, d = x.shape
    km = m.shape[1]
    ffn = w_out.shape[0]
    return pl.pallas_call(
        functools.partial(_post_kernel, ffn=ffn, ck=256),
        grid=(rows // tm,),
        in_specs=[
            pl.BlockSpec((tm, km), lambda i: (i, 0)),
            pl.BlockSpec((tm, d), lambda i: (i, 0)),
            _resident(wo.shape),
            _resident(gains.shape),
            _resident(w_in.shape),
            _resident(w_out.shape),
        ],
        out_specs=pl.BlockSpec((tm, d), lambda i: (i, 0)),
        out_shape=jax.ShapeDtypeStruct((rows, d), F32),
        compiler_params=_params(("parallel",)),
        name="post_mixer",
    )(m, x, wo, gains, w_in, w_out)


def _gmlp_kernel(x_ref, g_ref, win_ref, lng_ref, wm_ref, bs_ref, t_ref, *v_ref, nchunk, gd):
    h = _rms(x_ref[...], g_ref[0:1, :]).astype(BF16)
    u = jax.nn.gelu(_dot(h, win_ref[:, :gd]))
    vp = jax.nn.gelu(_dot(h, win_ref[:, gd:]))
    xc = vp - jnp.mean(vp, axis=-1, keepdims=True)
    v = xc * lax.rsqrt(jnp.mean(xc * xc, axis=-1, keepdims=True) + EPS) * lng_ref[...]
    if v_ref:
        v_ref[0][...] = v
    vb = v.astype(BF16)
    gw = gd // GMLP_GROUPS
    tri = (lax.broadcasted_iota(jnp.int32, (CHUNK, CHUNK), 0)
           >= lax.broadcasted_iota(jnp.int32, (CHUNK, CHUNK), 1))
    for gg in range(GMLP_GROUPS):
        wmg = jnp.where(tri, wm_ref[gg], 0.0).astype(BF16)
        bias = bs_ref[:, gg:gg + 1]
        for c in range(nchunk):
            rs = slice(c * CHUNK, (c + 1) * CHUNK)
            cs = slice(gg * gw, (gg + 1) * gw)
            mixed = _dot(wmg, vb[rs, cs]) + bias
            t_ref[rs, cs] = (u[rs, cs] * mixed).astype(BF16)


def _gmlp(x, gains, w_in, ln_g, wm, bs_t, tm, emit_v):
    rows, d = x.shape
    gd = w_in.shape[1] // 2
    out_shape = [jax.ShapeDtypeStruct((rows, gd), BF16)]
    out_specs = [pl.BlockSpec((tm, gd), lambda i: (i, 0))]
    if emit_v:
        out_shape.append(jax.ShapeDtypeStruct((rows, gd), F32))
        out_specs.append(pl.BlockSpec((tm, gd), lambda i: (i, 0)))
    return pl.pallas_call(
        functools.partial(_gmlp_kernel, nchunk=tm // CHUNK, gd=gd),
        grid=(rows // tm,),
        in_specs=[
            pl.BlockSpec((tm, d), lambda i: (i, 0)),
            _resident(gains.shape),
            _resident(w_in.shape),
            _resident(ln_g.shape),
            _resident(wm.shape),
            _resident(bs_t.shape),
        ],
        out_specs=out_specs,
        out_shape=out_shape,
        compiler_params=_params(("parallel",)),
        name="gmlp_mix",
    )(x, gains, w_in, ln_g, wm, bs_t)


def _nsa_proj_kernel(x_ref, g_ref, wq_ref, wkv_ref, wg_ref,
                     q_ref, kvc_ref, kvs_ref, kvw_ref, kvct_ref, kvst_ref, kvwt_ref, gate_ref,
                     ksel_ref, vselt_ref, kwin_ref, vwint_ref, *, tm, seq):
    h = _rms(x_ref[...], g_ref[0:1, :]).astype(BF16)
    q_ref[...] = (_dot(h, wq_ref[...]) * ATTN_SCALE).astype(BF16)
    kv = _dot(h, wkv_ref[...])
    kvc = kv[:, :KV_DIM]
    kvs = kv[:, KV_DIM:2 * KV_DIM]
    kvw = kv[:, 2 * KV_DIM:]
    kvc_ref[...] = kvc
    kvs_ref[...] = kvs
    kvw_ref[...] = kvw
    kvst = kvs.T
    kvwt = kvw.T
    kvct = kvc.T
    for p in range(tm // PAGE_SIZE):
        cols = slice(p * PAGE_SIZE, (p + 1) * PAGE_SIZE)
        kvct_ref[p] = kvct[:, cols]
        kvst_ref[p] = kvst[:, cols]
        kvwt_ref[p] = kvwt[:, cols]
    gz = _dot(h, wg_ref[...])
    for g in range(KV_GROUPS):
        gate_ref[g] = jax.nn.sigmoid(gz[:, g * LANES:(g + 1) * LANES])
    t0 = lax.rem(pl.program_id(0) * tm, seq)
    blk = (t0 + lax.broadcasted_iota(jnp.int32, (tm, LANES), 0)) // SEL_BLOCK
    onehot = jnp.where(lax.broadcasted_iota(jnp.int32, (tm, LANES), 1) == blk,
                       -MASK_BIG, 0.0).astype(BF16)
    ones = jnp.ones((HEAD_DIM, tm), BF16)
    for g in range(KV_GROUPS):
        ks = slice(g * HEAD_DIM, (g + 1) * HEAD_DIM)
        vs = slice(GKV + g * HEAD_DIM, GKV + (g + 1) * HEAD_DIM)
        ksel_ref[g, :, 0:LANES] = onehot
        ksel_ref[g, :, LANES:LANES + HEAD_DIM] = kvs[:, ks].astype(BF16)
        kwin_ref[g] = kvw[:, ks].astype(BF16)
        vselt_ref[g, 0:HEAD_DIM, :] = kvst[vs, :].astype(BF16)
        vselt_ref[g, HEAD_DIM:, :] = ones
        vwint_ref[g, 0:HEAD_DIM, :] = kvwt[vs, :].astype(BF16)
        vwint_ref[g, HEAD_DIM:, :] = ones


def _nsa_proj(x, gains, wq, wkv, wg, tm, seq):
    rows, d = x.shape
    row = lambda i: (i, 0)
    grow = lambda i: (0, i, 0)
    gcol = lambda i: (0, 0, i)
    kaug = LANES + HEAD_DIM
    npg = tm // PAGE_SIZE
    f32 = lambda *s: jax.ShapeDtypeStruct(s, F32)
    bf16 = lambda *s: jax.ShapeDtypeStruct(s, BF16)
    out_shape = [
        bf16(rows, Q_DIM),
        f32(rows, KV_DIM), f32(rows, KV_DIM), f32(rows, KV_DIM),
        f32(rows // PAGE_SIZE, KV_DIM, PAGE_SIZE), f32(rows // PAGE_SIZE, KV_DIM, PAGE_SIZE),
        f32(rows // PAGE_SIZE, KV_DIM, PAGE_SIZE),
        f32(KV_GROUPS, rows, LANES),
        bf16(KV_GROUPS, rows, kaug), bf16(KV_GROUPS, 2 * HEAD_DIM, rows),
        bf16(KV_GROUPS, rows, HEAD_DIM), bf16(KV_GROUPS, 2 * HEAD_DIM, rows),
    ]
    page = pl.BlockSpec((npg, KV_DIM, PAGE_SIZE), lambda i: (i, 0, 0))
    out_specs = [
        pl.BlockSpec((tm, Q_DIM), row),
        pl.BlockSpec((tm, KV_DIM), row), pl.BlockSpec((tm, KV_DIM), row),
        pl.BlockSpec((tm, KV_DIM), row),
        page, page, page,
        pl.BlockSpec((KV_GROUPS, tm, LANES), grow),
        pl.BlockSpec((KV_GROUPS, tm, kaug), grow),
        pl.BlockSpec((KV_GROUPS, 2 * HEAD_DIM, tm), gcol),
        pl.BlockSpec((KV_GROUPS, tm, HEAD_DIM), grow),
        pl.BlockSpec((KV_GROUPS, 2 * HEAD_DIM, tm), gcol),
    ]
    return pl.pallas_call(
        functools.partial(_nsa_proj_kernel, tm=tm, seq=seq),
        grid=(rows // tm,),
        in_specs=[
            pl.BlockSpec((tm, d), row),
            _resident(gains.shape),
            _resident(wq.shape),
            _resident(wkv.shape),
            _resident(wg.shape),
        ],
        out_specs=out_specs,
        out_shape=out_shape,
        compiler_params=_params(("parallel",)),
        name="nsa_proj",
    )(x, gains, wq, wkv, wg)


def _compress_ab_compute(xget, bd_ref, ab_ref):
    for kv in range(2):
        acc = None
        for l in range(CMP_STRIDE):
            part = _dot(xget(l, kv).astype(BF16), bd_ref[kv, l])
            acc = part if acc is None else acc + part
        ab_ref[:, kv * 2 * GKV:(kv + 1) * 2 * GKV] = acc


def _compress_ab_kernel(*refs, nrow):
    x_refs, (bd_ref, ab_ref) = refs[:-2], refs[-2:]
    per_kv = GKV // LANES

    def xget(l, kv):
        return jnp.concatenate([x_refs[kv * per_kv + h][pl.ds(l, nrow, stride=CMP_STRIDE), :]
                                for h in range(per_kv)], axis=1)

    _compress_ab_compute(xget, bd_ref, ab_ref)


def _compress_ab(x, bd, nb):
    seq = x.shape[0] // nb
    n = seq // CMP_STRIDE
    rb = min(n, 256)
    nr = n // rb
    ncol = KV_DIM // LANES
    xspecs = [pl.BlockSpec((rb * CMP_STRIDE, LANES), lambda b, r, j=j: (b * nr + r, j))
              for j in range(ncol)]
    return pl.pallas_call(
        functools.partial(_compress_ab_kernel, nrow=rb),
        grid=(nb, nr),
        in_specs=xspecs + [_resident(bd.shape)],
        out_specs=pl.BlockSpec((None, rb, 4 * GKV), lambda b, r: (b, r, 0)),
        out_shape=jax.ShapeDtypeStruct((nb, n, 4 * GKV), F32),
        compiler_params=_params(("parallel", "parallel")),
        name="compress_ab",
    )(*([x] * ncol), bd)


def _compress_ab_paged_kernel(pt_ref, pool_ref, perm_ref, bd_ref, ab_ref, xbuf, xl_ref, sem,
                              *, pgs, nsplit):
    s = pl.program_id(0)
    nsteps = pl.num_programs(0)

    def copies(step, slot):
        b = step // nsplit
        h = step % nsplit
        return [pltpu.make_async_copy(pool_ref.at[pt_ref[b, h * pgs + p]], xbuf.at[slot, p],
                                      sem.at[slot]) for p in range(pgs)]

    @pl.when(s == 0)
    def _():
        for c in copies(s, 0):
            c.start()

    @pl.when(s + 1 < nsteps)
    def _():
        for c in copies(s + 1, (s + 1) % 2):
            c.start()

    slot = s % 2
    for c in copies(s, slot):
        c.wait()

    rows_pp = PAGE_SIZE // CMP_STRIDE
    perm = perm_ref[...]

    def body(p, carry):
        r0 = pl.multiple_of(p * rows_pp, rows_pp)
        for kv in range(2):
            y = _dot_nt(perm, xbuf[slot, p, kv].astype(BF16))
            for l in range(CMP_STRIDE):
                xl_ref[l, pl.ds(r0, rows_pp), kv * GKV:(kv + 1) * GKV] = (
                    y[l * rows_pp:(l + 1) * rows_pp, :])
        return carry

    lax.fori_loop(0, pgs, body, 0, unroll=8)
    _compress_ab_compute(lambda l, kv: xl_ref[l, :, kv * GKV:(kv + 1) * GKV], bd_ref, ab_ref)


def _compress_ab_paged(pt, pool_t, bd, pgs):
    nb, n_pages = pt.shape
    nsplit = n_pages // pgs
    rows_pp = PAGE_SIZE // CMP_STRIDE
    rb = pgs * rows_pp
    ln = jnp.arange(PAGE_SIZE)
    perm = (jnp.arange(PAGE_SIZE)[None, :]
            == (CMP_STRIDE * (ln % rows_pp) + ln // rows_pp)[:, None]).astype(BF16)
    grid_spec = pltpu.PrefetchScalarGridSpec(
        num_scalar_prefetch=1,
        grid=(nb * nsplit,),
        in_specs=[pl.BlockSpec(memory_space=pl.ANY),
                  pl.BlockSpec(perm.shape, lambda s, pt: (0, 0), pipeline_mode=pl.Buffered(1)),
                  pl.BlockSpec(bd.shape, lambda s, pt: (0,) * 4, pipeline_mode=pl.Buffered(1))],
        out_specs=pl.BlockSpec((None, rb, 4 * GKV), lambda s, pt: (s // nsplit, s % nsplit, 0)),
        scratch_shapes=[pltpu.VMEM((2, pgs, 2, GKV, PAGE_SIZE), F32),
                        pltpu.VMEM((CMP_STRIDE, rb, KV_DIM), F32),
                        pltpu.SemaphoreType.DMA((2,))],
    )
    return pl.pallas_call(
        functools.partial(_compress_ab_paged_kernel, pgs=pgs, nsplit=nsplit),
        grid_spec=grid_spec,
        out_shape=jax.ShapeDtypeStruct((nb, n_pages * rows_pp, 4 * GKV), F32),
        compiler_params=_params(("arbitrary",)),
        name="compress_ab_paged",
    )(pt, pool_t, perm, bd)


def _compress_fin_kernel(ab_ref, pe_ref, w1_ref, w2_ref, kc_ref, vct_ref, *, n):
    for kv in range(2):
        a = ab_ref[:, kv * 2 * GKV:kv * 2 * GKV + GKV]
        b = ab_ref[:, kv * 2 * GKV + GKV:(kv + 1) * 2 * GKV]
        c = _dot(pe_ref[kv], w1_ref[kv])[0:1, :]
        c4 = jnp.concatenate([c] * KV_GROUPS, axis=1)
        hid = a + pltpu.roll(b, n - 1, 0) + c4
        o = _dot(jax.nn.gelu(hid).astype(BF16), w2_ref[kv])
        if kv == 0:
            for g in range(KV_GROUPS):
                kc_ref[g] = o[:, g * HEAD_DIM:(g + 1) * HEAD_DIM].astype(BF16)
        else:
            ot = o.T
            for g in range(KV_GROUPS):
                vct_ref[g] = ot[g * HEAD_DIM:(g + 1) * HEAD_DIM, :].astype(BF16)


def _compress_fin(ab, pe8, w1f, w2bd):
    nb, n, _ = ab.shape
    return pl.pallas_call(
        functools.partial(_compress_fin_kernel, n=n),
        grid=(nb,),
        in_specs=[pl.BlockSpec((None, n, 4 * GKV), lambda b: (b, 0, 0)),
                  _resident(pe8.shape), _resident(w1f.shape), _resident(w2bd.shape)],
        out_specs=[pl.BlockSpec((None, KV_GROUPS, n, HEAD_DIM), lambda b: (b, 0, 0, 0)),
                   pl.BlockSpec((None, KV_GROUPS, HEAD_DIM, n), lambda b: (b, 0, 0, 0))],
        out_shape=[jax.ShapeDtypeStruct((nb, KV_GROUPS, n, HEAD_DIM), BF16),
                   jax.ShapeDtypeStruct((nb, KV_GROUPS, HEAD_DIM, n), BF16)],
        compiler_params=_params(("parallel",)),
        name="compress_fin",
    )(ab, pe8, w1f, w2bd)


def _group_rows(qt):
    return jnp.concatenate([qt[:, r * HEAD_DIM:(r + 1) * HEAD_DIM] for r in range(GROUP_SIZE)],
                           axis=0)


def _ungroup_t(ot, tq):
    return jnp.concatenate([ot[:, r * tq:(r + 1) * tq].T for r in range(GROUP_SIZE)], axis=1)


def _lane_qpos(s0, tq):
    q = lax.broadcasted_iota(jnp.int32, (1, tq), 1)
    return s0 + jnp.concatenate([q] * GROUP_SIZE, axis=1)


def _gate_lanes(gtt, j, tq):
    return jnp.concatenate([gtt[3 * r + j:3 * r + j + 1, :] for r in range(GROUP_SIZE)], axis=1)


def _topk_mask(score, k):
    nb = score.shape[0]
    idx = lax.broadcasted_iota(jnp.int32, score.shape, 0).astype(F32)

    def body(_, carry):
        work, sel = carry
        m = jnp.max(work, axis=0, keepdims=True)
        first = jnp.min(jnp.where(work == m, idx, float(nb)), axis=0, keepdims=True)
        pick = idx == first
        return jnp.where(pick, -3e38, work), jnp.where(pick, 1.0, sel)

    _, sel = lax.fori_loop(0, k, body, (score, jnp.zeros_like(score)), unroll=True)
    return sel


def _cmp_select_kernel(q_ref, kc_ref, vct_ref, ovt_ref, gate_ref, oc_ref, nsel_ref,
                       *, tq, pos0, k_top):
    s0 = pos0 + pl.program_id(2) * tq
    qrows = _group_rows(q_ref[...])
    n = kc_ref.shape[0]
    st = _dot_nt(kc_ref[...], qrows)
    qpos = _lane_qpos(s0, tq)
    end = lax.broadcasted_iota(jnp.int32, (n, 1), 0) * CMP_STRIDE + (CMP_BLOCK - 1)
    sm = jnp.where(end <= qpos, st, NEG)
    e = jnp.exp(sm - jnp.max(sm, axis=0, keepdims=True))
    p = e * jnp.where(qpos >= CMP_BLOCK - 1, 1.0 / jnp.sum(e, axis=0, keepdims=True), 0.0)
    oct_ = _dot(vct_ref[...], p.astype(BF16))
    gtt = gate_ref[...].T
    oc_ref[...] = _ungroup_t(_gate_lanes(gtt, 0, tq) * oct_, tq).astype(BF16)
    psum = p[:, 0:tq]
    for r in range(1, GROUP_SIZE):
        psum = psum + p[:, r * tq:(r + 1) * tq]
    hi = psum.astype(BF16)
    lo = (psum - hi.astype(F32)).astype(BF16)
    imp_t = _dot(ovt_ref[...], hi) + _dot(ovt_ref[...], lo)
    shape = imp_t.shape
    blk = lax.broadcasted_iota(jnp.int32, shape, 0)
    cur = (s0 + lax.broadcasted_iota(jnp.int32, shape, 1)) // SEL_BLOCK
    valid = blk <= cur
    forced = (blk == 0) | (blk > cur - N_LOCAL)
    score = jnp.where(valid, jnp.where(forced, FORCE, imp_t), NEG)
    sel = _topk_mask(score, k_top)
    nsel_t = jnp.where((sel > 0.5) & valid, 0.0, 1.0)
    nsel_ref[...] = nsel_t.T.astype(BF16)


def _cmp_select(q, kc, vct, ovt, gates, tpad, pos0, k_top):
    nb, _, n, _ = kc.shape
    tq = min(512, tpad)
    nq = tpad // tq
    nbp = ovt.shape[0]
    gq = GROUP_SIZE * HEAD_DIM
    return pl.pallas_call(
        functools.partial(_cmp_select_kernel, tq=tq, pos0=pos0, k_top=k_top),
        grid=(nb, KV_GROUPS, nq),
        in_specs=[
            pl.BlockSpec((tq, gq), lambda b, g, i: (b * nq + i, g)),
            pl.BlockSpec((None, None, n, HEAD_DIM), lambda b, g, i: (b, g, 0, 0)),
            pl.BlockSpec((None, None, HEAD_DIM, n), lambda b, g, i: (b, g, 0, 0)),
            pl.BlockSpec(ovt.shape, lambda b, g, i: (0, 0)),
            pl.BlockSpec((None, tq, LANES), lambda b, g, i: (g, b * nq + i, 0)),
        ],
        out_specs=[
            pl.BlockSpec((tq, gq), lambda b, g, i: (b * nq + i, g)),
            pl.BlockSpec((None, None, tq, nbp), lambda b, g, i: (b, g, i, 0)),
        ],
        out_shape=[
            jax.ShapeDtypeStruct((nb * tpad, Q_DIM), BF16),
            jax.ShapeDtypeStruct((nb, KV_GROUPS, tpad, nbp), BF16),
        ],
        compiler_params=_params(("parallel", "parallel", "parallel")),
        name="cmp_select",
    )(q, kc, vct, ovt, gates)


def _sel_win_kernel(q_ref, nsel_ref, gate_ref, oc_ref, ksel_ref, vselt_ref, kwin_ref, vwint_ref,
                    o_ref, sa_ref, sb_ref, m_ref, acc_ref, *, ts, nsub, tk, seq):
    s00 = pl.program_id(2) * (ts * nsub)
    cols = GROUP_SIZE * ts
    subs = []
    for u in range(nsub):
        rs = slice(u * ts, (u + 1) * ts)
        qrows = _group_rows(q_ref[rs, :])
        ns = nsel_ref[rs, :]
        qaug = jnp.concatenate([jnp.concatenate([ns] * GROUP_SIZE, axis=0), qrows], axis=1)
        subs.append((rs, s00 + u * ts, qrows, qaug, _lane_qpos(s00 + u * ts, ts),
                     gate_ref[rs, :].T))
        m_ref[u] = jnp.full((1, cols), NEG, F32)
        acc_ref[u] = jnp.zeros((2 * HEAD_DIM, cols), F32)

    def scores(ref, j):
        k0 = pl.multiple_of(j * tk, tk)
        for u in range(nsub):
            ref[u] = _dot_nt(ksel_ref[pl.ds(k0, tk), :], subs[u][3])

    def update(ref, j, causal):
        k0 = pl.multiple_of(j * tk, tk)
        for u in range(nsub):
            st = ref[u]
            if causal:
                kpos = k0 + lax.broadcasted_iota(jnp.int32, (tk, 1), 0)
                st = jnp.where(kpos <= subs[u][4], st, NEG)
            m = m_ref[u]
            mn = jnp.maximum(m, jnp.max(st, axis=0, keepdims=True))
            p = jnp.exp(st - mn).astype(BF16)
            acc_ref[u] = jnp.exp(m - mn) * acc_ref[u] + _dot(vselt_ref[:, pl.ds(k0, tk)], p)
            m_ref[u] = mn

    jd = s00 // tk
    scores(sa_ref, 0)

    def body(i, carry):
        j = 2 * i
        scores(sb_ref, j + 1)
        update(sa_ref, j, False)
        scores(sa_ref, j + 2)
        update(sb_ref, j + 1, False)
        return carry

    lax.fori_loop(0, jd // 2, body, 0)
    odd = lax.rem(jd, 2) == 1

    @pl.when(odd)
    def _():
        scores(sb_ref, jd)
        update(sa_ref, jd - 1, False)
        update(sb_ref, jd, True)

    @pl.when(jnp.logical_not(odd))
    def _():
        update(sa_ref, jd, True)

    nw = min(WINDOW + ts, seq)
    for u, (rs, s0, qrows, _, qpos, gtt) in enumerate(subs):
        acc = acc_ref[u]
        o_t = _gate_lanes(gtt, 1, ts) * (acc[:HEAD_DIM] / acc[HEAD_DIM:HEAD_DIM + 1])
        w0 = pl.multiple_of(jnp.maximum(s0 + ts - nw, 0), ts)
        sw = _dot_nt(kwin_ref[pl.ds(w0, nw), :], qrows)
        d = qpos - (w0 + lax.broadcasted_iota(jnp.int32, (nw, 1), 0))
        sw = jnp.where((d >= 0) & (d < WINDOW), sw, NEG)
        pw = jnp.exp(sw - jnp.max(sw, axis=0, keepdims=True)).astype(BF16)
        accw = _dot(vwint_ref[:, pl.ds(w0, nw)], pw)
        o_t = o_t + _gate_lanes(gtt, 2, ts) * (accw[:HEAD_DIM] / accw[HEAD_DIM:HEAD_DIM + 1])
        o_ref[rs, :] = (oc_ref[rs, :].astype(F32) + _ungroup_t(o_t, ts)).astype(BF16)


def _sel_win(q, nsel, gates, ocg, ksel, vselt, kwin, vwint, nb, seq):
    ts = 128
    tk = min(512, seq)
    nsub = 4
    assert tk % (ts * nsub) == 0
    tq = ts * nsub
    nq = seq // tq
    gq = GROUP_SIZE * HEAD_DIM
    qspec = pl.BlockSpec((tq, gq), lambda b, g, i: (b * nq + i, g))
    krows = lambda w: pl.BlockSpec((None, seq, w), lambda b, g, i: (g, b, 0))
    vcols = pl.BlockSpec((None, 2 * HEAD_DIM, seq), lambda b, g, i: (g, 0, b))
    return pl.pallas_call(
        functools.partial(_sel_win_kernel, ts=ts, nsub=nsub, tk=tk, seq=seq),
        grid=(nb, KV_GROUPS, nq),
        in_specs=[
            qspec,
            pl.BlockSpec((None, None, tq, LANES), lambda b, g, i: (b, g, i, 0)),
            pl.BlockSpec((None, tq, LANES), lambda b, g, i: (g, b * nq + i, 0)),
            qspec,
            krows(LANES + HEAD_DIM), vcols, krows(HEAD_DIM), vcols,
        ],
        out_specs=qspec,
        out_shape=jax.ShapeDtypeStruct((nb * seq, Q_DIM), BF16),
        scratch_shapes=[pltpu.VMEM((nsub, tk, GROUP_SIZE * ts), F32),
                        pltpu.VMEM((nsub, tk, GROUP_SIZE * ts), F32),
                        pltpu.VMEM((nsub, 1, GROUP_SIZE * ts), F32),
                        pltpu.VMEM((nsub, 2 * HEAD_DIM, GROUP_SIZE * ts), F32)],
        compiler_params=_params(("parallel", "parallel", "arbitrary")),
        name="sel_win_attn",
    )(q, nsel, gates, ocg, ksel, vselt, kwin, vwint)


def _sample_attn_kernel(pt_ref, pool_ref, qbd_ref, nselr_ref, eneg_ref, knew_ref, wnew_ref,
                        wbuf_ref, gate_ref, ocg_ref, o_ref,
                        kbuf, sem, m_sc, l_sc, acc_sc, *, pgs, nch, ppt, past, tn):
    b = pl.program_id(0)
    c = pl.program_id(1)
    step = b * nch + c
    nsteps = pl.num_programs(0) * nch

    def copies(st, slot):
        bb = st // nch
        cc = st % nch
        return [pltpu.make_async_copy(pool_ref.at[pt_ref[bb, cc * pgs + p]], kbuf.at[slot, p],
                                      sem.at[slot]) for p in range(pgs)]

    @pl.when(step == 0)
    def _():
        for cp in copies(step, 0):
            cp.start()

    @pl.when(step + 1 < nsteps)
    def _():
        for cp in copies(step + 1, (step + 1) % 2):
            cp.start()

    @pl.when(c == 0)
    def _():
        m_sc[...] = jnp.full(m_sc.shape, NEG, F32)
        l_sc[...] = jnp.zeros(l_sc.shape, F32)
        acc_sc[...] = jnp.zeros(acc_sc.shape, F32)

    slot = step % 2
    for cp in copies(step, slot):
        cp.wait()

    qbd = qbd_ref[...]
    nselr = nselr_ref[...]
    rows = qbd.shape[0]
    qpos = past + lax.rem(lax.broadcasted_iota(jnp.int32, (rows, 1), 0), tn)

    def online(s, pv):
        m = m_sc[...]
        mn = jnp.maximum(m, jnp.max(s, axis=-1, keepdims=True))
        p = jnp.exp(s - mn)
        alpha = jnp.exp(m - mn)
        l_sc[...] = alpha * l_sc[...] + jnp.sum(p, axis=-1, keepdims=True)
        acc_sc[...] = alpha * acc_sc[...] + pv(p.astype(BF16))
        m_sc[...] = mn

    tk = ppt * PAGE_SIZE
    for t in range(pgs // ppt):
        kt = jnp.concatenate([kbuf[slot, t * ppt + i, 0] for i in range(ppt)], axis=1).astype(BF16)
        vt = jnp.concatenate([kbuf[slot, t * ppt + i, 1] for i in range(ppt)], axis=1).astype(BF16)
        k0 = pl.multiple_of(c * (pgs * PAGE_SIZE) + t * tk, tk)
        bias = _dot_nt(nselr, eneg_ref[pl.ds(k0, tk), :])
        online(_dot(qbd, kt) + bias, lambda p, vt=vt: _dot_nt(p, vt))

    @pl.when(c == nch - 1)
    def _():
        npad = knew_ref.shape[0]
        newpos = past + lax.broadcasted_iota(jnp.int32, (1, npad), 1)
        new_ok = (newpos <= qpos) & (newpos < past + tn)
        kn = knew_ref[:, 0:GKV].astype(BF16)
        vn = knew_ref[:, GKV:].astype(BF16)
        online(jnp.where(new_ok, _dot_nt(qbd, kn), NEG), lambda p: _dot(p, vn))
        o_s = acc_sc[...] / l_sc[...]
        wbl = wbuf_ref.shape[2]
        dw = qpos - (past - wbl + lax.broadcasted_iota(jnp.int32, (1, wbl), 1))
        s1 = jnp.where((dw >= 0) & (dw < WINDOW), _dot(qbd, wbuf_ref[0].astype(BF16)), NEG)
        s2 = jnp.where(new_ok & (qpos - newpos < WINDOW),
                       _dot_nt(qbd, wnew_ref[:, 0:GKV].astype(BF16)), NEG)
        mw = jnp.maximum(jnp.max(s1, axis=-1, keepdims=True), jnp.max(s2, axis=-1, keepdims=True))
        p1 = jnp.exp(s1 - mw)
        p2 = jnp.exp(s2 - mw)
        lw = jnp.sum(p1, axis=-1, keepdims=True) + jnp.sum(p2, axis=-1, keepdims=True)
        o_w = (_dot_nt(p1.astype(BF16), wbuf_ref[1].astype(BF16))
               + _dot(p2.astype(BF16), wnew_ref[:, GKV:].astype(BF16))) / lw
        gt = gate_ref[...]
        o_ref[...] = ocg_ref[...] + gt[:, 1:2] * o_s + gt[:, 2:3] * o_w


def _sample_attn(pt, pool_t, qbd, nselr, eneg, knew, wnew, wbuf_t, wbuf_off, gates_r, ocg_r,
                 pgs, past, tn):
    nb, n_pages = pt.shape
    nch = n_pages // pgs
    rows = qbd.shape[1]
    ppt = min(4, pgs)
    per_b = lambda shape: pl.BlockSpec((None,) + shape, lambda b, c, pt: (b,) + (0,) * len(shape))
    grid_spec = pltpu.PrefetchScalarGridSpec(
        num_scalar_prefetch=1,
        grid=(nb, nch),
        in_specs=[
            pl.BlockSpec(memory_space=pl.ANY),
            per_b(qbd.shape[1:]),
            per_b(nselr.shape[1:]),
            pl.BlockSpec(eneg.shape, lambda b, c, pt: (0, 0), pipeline_mode=pl.Buffered(1)),
            per_b(knew.shape[1:]),
            per_b(wnew.shape[1:]),
            pl.BlockSpec((None,) + wbuf_t.shape[1:], lambda b, c, pt: (wbuf_off + b, 0, 0, 0)),
            per_b(gates_r.shape[1:]),
            per_b(ocg_r.shape[1:]),
        ],
        out_specs=per_b((rows, GKV)),
        scratch_shapes=[
            pltpu.VMEM((2, pgs, 2, GKV, PAGE_SIZE), F32),
            pltpu.SemaphoreType.DMA((2,)),
            pltpu.VMEM((rows, 1), F32),
            pltpu.VMEM((rows, 1), F32),
            pltpu.VMEM((rows, GKV), F32),
        ],
    )
    return pl.pallas_call(
        functools.partial(_sample_attn_kernel, pgs=pgs, nch=nch, ppt=ppt, past=past, tn=tn),
        grid_spec=grid_spec,
        out_shape=jax.ShapeDtypeStruct((nb, rows, GKV), F32),
        compiler_params=_params(("arbitrary", "arbitrary")),
        name="sample_attn",
    )(pt, pool_t, qbd, nselr, eneg, knew, wnew, wbuf_t, gates_r, ocg_r)


def _overlap_t(nbp, nrow):
    cs = jnp.arange(nrow)[None, :] * CMP_STRIDE
    ss = jnp.arange(nbp)[:, None] * SEL_BLOCK
    ov = jnp.minimum(cs + CMP_BLOCK, ss + SEL_BLOCK) - jnp.maximum(cs, ss)
    return (jnp.clip(ov, 0).astype(F32) / CMP_BLOCK).astype(BF16)


def _block_diag_w1(w1):
    eye = jnp.eye(KV_GROUPS, dtype=w1.dtype)
    bd = jnp.einsum('gh,klde->klgdhe', eye, w1).reshape(2, CMP_BLOCK, GKV, GKV)
    return jnp.concatenate([bd[:, :CMP_STRIDE], bd[:, CMP_STRIDE:]], axis=-1).astype(BF16)


def _block_diag_w2(w2):
    eye = jnp.eye(KV_GROUPS, dtype=w2.dtype)
    return jnp.einsum('gh,kde->kgdhe', eye, w2).reshape(2, GKV, GKV).astype(BF16)


def _gate_weight(wg):
    d = wg.shape[0]
    w = wg.reshape(d, KV_GROUPS, 3 * GROUP_SIZE)
    w = jnp.pad(w, ((0, 0), (0, 0), (0, LANES - 3 * GROUP_SIZE)))
    return w.reshape(d, KV_GROUPS * LANES).astype(BF16)


def _token_minor(cache):
    nd = cache.ndim
    perm = tuple(range(nd - 4)) + (nd - 3, nd - 2, nd - 1, nd - 4)
    t = cache.transpose(perm)
    return t.reshape(t.shape[:nd - 3] + (GKV, t.shape[-1]))


def _from_pages(pages_t, lead):
    x = pages_t.reshape(lead + (2, KV_GROUPS, HEAD_DIM, PAGE_SIZE))
    nd = x.ndim
    return x.transpose(tuple(range(nd - 4)) + (nd - 1, nd - 4, nd - 3, nd - 2))


def _nsa_layer(yp, ys, a, cache_cmp_kv, cache_sel_kv, cache_win_kv, page_table, gains,
               w_nsa_in, w_cmp_hidden, w_cmp_out, cmp_pos_emb, dims):
    bsz, seq, db, tn, past = dims
    n_pool = cache_cmp_kv.shape[1]
    n_pages = page_table.shape[1]
    w_in = w_nsa_in[a]
    wq = w_in[:, :Q_DIM].astype(BF16)
    wkv = w_in[:, Q_DIM:Q_DIM + 3 * KV_DIM].astype(BF16)
    wg = _gate_weight(w_in[:, Q_DIM + 3 * KV_DIM:])
    bd1 = _block_diag_w1(w_cmp_hidden[a])
    w2bd = _block_diag_w2(w_cmp_out[a])
    w1f = w_cmp_hidden[a].reshape(2, CMP_BLOCK * HEAD_DIM, HEAD_DIM).astype(BF16)
    pe8 = jnp.broadcast_to(cmp_pos_emb[a].reshape(2, 1, CMP_BLOCK * HEAD_DIM),
                           (2, 8, CMP_BLOCK * HEAD_DIM)).astype(BF16)

    tm = min(512, seq)
    (qp, kvc_p, _, _, kvct_p, kvst_p, kvwt_p, gate_p, ksel, vselt, kwin, vwint) = _nsa_proj(
        yp, gains, wq, wkv, wg, tm, seq)
    nrow = seq // CMP_STRIDE
    kc, vct = _compress_fin(_compress_ab(kvc_p, bd1, bsz), pe8, w1f, w2bd)
    ns_p = -(-seq // SEL_BLOCK)
    assert ns_p <= LANES
    ocg, nsel = _cmp_select(qp, kc, vct, _overlap_t(LANES, nrow), gate_p, seq, 0,
                            min(SEL_TOPN, ns_p))
    op = _sel_win(qp, nsel, gate_p, ocg, ksel, vselt, kwin, vwint, bsz, seq)

    rows_s = db * tn
    qs, kvc_s, kvs_s, kvw_s, _, _, _, gate_s, _, _, _, _ = _nsa_proj(
        ys, gains, wq, wkv, wg, rows_s, rows_s)
    pt_abs = page_table + a * n_pool
    pool_c = _token_minor(cache_cmp_kv).reshape(-1, 2, GKV, PAGE_SIZE)
    pool_s = _token_minor(cache_sel_kv).reshape(-1, 2, GKV, PAGE_SIZE)
    ab_s = _compress_ab_paged(pt_abs, pool_c, bd1, min(32, n_pages))
    kc_s, vct_s = _compress_fin(ab_s, pe8, w1f, w2bd)
    nrow_s = past // CMP_STRIDE
    ns_s = -(-(past + tn) // SEL_BLOCK)
    nbp_s = -(-ns_s // LANES) * LANES
    tpad = 128
    qs_pad = jnp.pad(qs.reshape(db, tn, Q_DIM), ((0, 0), (0, tpad - tn), (0, 0)))
    gate_pad = jnp.pad(gate_s.reshape(KV_GROUPS, db, tn, LANES),
                       ((0, 0), (0, 0), (0, tpad - tn), (0, 0)))
    ocg_s, nsel_s = _cmp_select(qs_pad.reshape(db * tpad, Q_DIM), kc_s, vct_s,
                                _overlap_t(nbp_s, nrow_s),
                                gate_pad.reshape(KV_GROUPS, db * tpad, LANES), tpad, past,
                                min(SEL_TOPN, ns_s))
    eye = jnp.eye(KV_GROUPS, dtype=BF16)
    q5 = qs.reshape(db, tn, KV_GROUPS, GROUP_SIZE, HEAD_DIM)
    rows = KV_GROUPS * GROUP_SIZE * tn
    qbd = jnp.einsum('bqgrd,gh->bgrqhd', q5, eye).reshape(db, rows, GKV)
    nselr = jnp.broadcast_to(nsel_s[:, :, None, :tn, :LANES],
                             (db, KV_GROUPS, GROUP_SIZE, tn, LANES)).reshape(db, rows, LANES)
    g4 = gate_s.reshape(KV_GROUPS, db, tn, LANES)[..., :3 * GROUP_SIZE]
    g4 = g4.reshape(KV_GROUPS, db, tn, GROUP_SIZE, 3).transpose(1, 0, 3, 2, 4)
    gates_r = jnp.pad(g4.reshape(db, rows, 3), ((0, 0), (0, 0), (0, LANES - 3)))
    oc5 = ocg_s.reshape(db, tpad, KV_GROUPS, GROUP_SIZE, HEAD_DIM)[:, :tn].astype(F32)
    ocg_r = jnp.einsum('bqgrd,gh->bgrqhd', oc5, jnp.eye(KV_GROUPS, dtype=F32)).reshape(
        db, rows, GKV)
    kpos = jnp.arange(past)[:, None] // SEL_BLOCK
    eneg = jnp.where(kpos == jnp.arange(LANES)[None, :], -MASK_BIG, 0.0).astype(BF16)
    npad = 128
    knew = jnp.pad(kvs_s.reshape(db, tn, KV_DIM), ((0, 0), (0, npad - tn), (0, 0)))
    wnew = jnp.pad(kvw_s.reshape(db, tn, KV_DIM), ((0, 0), (0, npad - tn), (0, 0)))
    wbl = cache_win_kv.shape[2]
    assert past >= wbl
    wbuf_t = _token_minor(cache_win_kv).reshape(-1, 2, GKV, wbl)
    o_rows = _sample_attn(pt_abs, pool_s, qbd, nselr, eneg, knew, wnew, wbuf_t, a * db,
                          gates_r, ocg_r, min(16, n_pages), past, tn)
    o6 = o_rows.reshape(db, KV_GROUPS, GROUP_SIZE, tn, KV_GROUPS, HEAD_DIM)
    o_s = jnp.einsum('bgrqhd,gh->bqgrd', o6, jnp.eye(KV_GROUPS, dtype=F32))
    o_s = o_s.reshape(rows_s, Q_DIM).astype(BF16)

    shp = (KV_GROUPS, HEAD_DIM)
    npg = seq // PAGE_SIZE
    wpg = min(WINDOW, seq) // PAGE_SIZE
    p_win = _from_pages(kvwt_p.reshape(bsz, npg, KV_DIM, PAGE_SIZE)[:, npg - wpg:], (bsz, wpg))
    kvw_s5 = kvw_s.reshape((db, tn, 2) + shp)
    win_new = jnp.concatenate([cache_win_kv[a], kvw_s5], axis=1)[:, tn:]
    caches = (_from_pages(kvct_p, (bsz, npg)), _from_pages(kvst_p, (bsz, npg)),
              p_win.reshape((bsz, wpg * PAGE_SIZE, 2) + shp),
              kvc_s.reshape((db, tn, 2) + shp), kvs_s.reshape((db, tn, 2) + shp), win_new)
    return op, o_s, caches


def kernel(x_prompt, x_sample, cache_cmp_kv, cache_sel_kv, cache_win_kv, page_table, norm_gains,
           w_nsa_in, w_cmp_hidden, w_cmp_out, cmp_pos_emb, w_nsa_out, w_gm_in, gm_norm_gain,
           w_spatial, b_spatial, w_gm_out, w_ffn_in, w_ffn_out):
    bsz, seq, d = x_prompt.shape
    db, tn, _ = x_sample.shape
    past = page_table.shape[1] * PAGE_SIZE
    depth = norm_gains.shape[0]
    assert seq % CHUNK == 0 and past % SEL_BLOCK == 0 and tn < CMP_STRIDE
    assert (db * tn) % 8 == 0 and CHUNK % tn == 0 and db * tn == CHUNK
    dims = (bsz, seq, db, tn, past)
    yp = x_prompt.reshape(bsz * seq, d)
    ys = x_sample.reshape(db * tn, d)
    tm_p = min(512, seq)
    tm_s = db * tn
    lists = [[] for _ in range(7)]
    for i in range(depth):
        gains = norm_gains[i]
        w_fin = w_ffn_in[i].astype(BF16)
        w_fout = w_ffn_out[i].astype(BF16)
        if i % 2 == 0:
            a = i // 2
            mp, ms, caches = _nsa_layer(yp, ys, a, cache_cmp_kv, cache_sel_kv, cache_win_kv,
                                        page_table, gains, w_nsa_in, w_cmp_hidden, w_cmp_out,
                                        cmp_pos_emb, dims)
            for lst, c in zip(lists[:6], caches):
                lst.append(c)
            wo = w_nsa_out[a].astype(BF16)
        else:
            bi = i // 2
            w_in = w_gm_in[bi].astype(BF16)
            ln_g = gm_norm_gain[bi].reshape(1, -1)
            ws = w_spatial[bi]
            bs = b_spatial[bi]
            mp = _gmlp(yp, gains, w_in, ln_g, ws, bs.T, tm_p, False)[0]
            eye = jnp.eye(db, dtype=ws.dtype)
            ws_s = jnp.einsum('bc,gts->gbtcs', eye, ws[:, :tn, :tn]).reshape(-1, tm_s, tm_s)
            bs_s = jnp.tile(bs[:, :tn], (1, db)).T
            ms, v_new = _gmlp(ys, gains, w_in, ln_g, ws_s, bs_s, tm_s, True)
            lists[6].append(v_new.reshape(db, tn, -1))
            wo = w_gm_out[bi].astype(BF16)
        yp = _post(mp, yp, wo, gains, w_fin, w_fout, tm_p)
        ys = _post(ms, ys, wo, gains, w_fin, w_fout, tm_s)
    return (yp.reshape(bsz, seq, d), ys.reshape(db, tn, d)) + tuple(jnp.stack(l) for l in lists)
```

```python
import functools

import jax
import jax.numpy as jnp
import numpy as np
from jax import lax
from jax.experimental import pallas as pl
from jax.experimental.pallas import tpu as pltpu

F32 = jnp.float32
BF16 = jnp.bfloat16

HEAD_DIM = 64
KV_GROUPS = 4
GROUP_SIZE = 4
N_HEADS = KV_GROUPS * GROUP_SIZE
Q_DIM = N_HEADS * HEAD_DIM
GKV = KV_GROUPS * HEAD_DIM
KV_DIM = 2 * GKV
CMP_STRIDE = 16
CMP_BLOCK = 2 * CMP_STRIDE
SEL_BLOCK = 64
SEL_TOPN = 16
N_LOCAL = 2
WINDOW = 512
PAGE_SIZE = 128
CHUNK = 128
GMLP_GROUPS = 8
ATTN_SCALE = HEAD_DIM ** -0.5
EPS = 1e-6
NEG = -1e30
FORCE = 1e4
MASK_BIG = 1e30
LANES = 128
VMEM_LIMIT = 56 * 1024 * 1024


def _params(sem):
    return pltpu.CompilerParams(dimension_semantics=sem, vmem_limit_bytes=VMEM_LIMIT)


def _resident(shape):
    nd = len(shape)
    return pl.BlockSpec(shape, lambda *_: (0,) * nd, pipeline_mode=pl.Buffered(1))


def _rms(x, g):
    return x * lax.rsqrt(jnp.mean(x * x, axis=-1, keepdims=True) + EPS) * g


def _dot(a, b):
    return jnp.dot(a, b, preferred_element_type=F32)


def _dot_nt(a, b):
    return lax.dot_general(a, b, (((1,), (1,)), ((), ())), preferred_element_type=F32)


def _post_kernel(m_ref, x_ref, wo_ref, g_ref, win_ref, wout_ref, y_ref, *, ffn, ck):
    x = x_ref[...]
    y1 = x + _rms(_dot(m_ref[...], wo_ref[...]), g_ref[1:2, :])
    h = _rms(y1, g_ref[2:3, :]).astype(BF16)
    acc = jnp.zeros_like(x)
    for c0 in range(0, ffn, ck):
        w = min(ck, ffn - c0)
        a = _dot(h, win_ref[:, c0:c0 + w])
        b = _dot(h, win_ref[:, ffn + c0:ffn + c0 + w])
        act = (jax.nn.silu(a) * b).astype(BF16)
        acc = acc + _dot(act, wout_ref[c0:c0 + w, :])
    y_ref[...] = y1 + _rms(acc, g_ref[3:4, :])


def _layer_of(stacked, layer):
    return pl.BlockSpec((None,) + stacked.shape[1:], lambda *_: (layer, 0, 0),
                        pipeline_mode=pl.Buffered(1))


def _post(m, x, wo, gains, w_in, w_out, layer, tm):
    rows, d = x.shape
    km = m.shape[1]
    ffn = w_out.shape[1]
    return pl.pallas_call(
        functools.partial(_post_kernel, ffn=ffn, ck=256),
        grid=(rows // tm,),
        in_specs=[
            pl.BlockSpec((tm, km), lambda i: (i, 0)),
            pl.BlockSpec((tm, d), lambda i: (i, 0)),
            _resident(wo.shape),
            _resident(gains.shape),
            _layer_of(w_in, layer),
            _layer_of(w_out, layer),
        ],
        out_specs=pl.BlockSpec((tm, d), lambda i: (i, 0)),
        out_shape=jax.ShapeDtypeStruct((rows, d), F32),
        compiler_params=_params(("parallel",)),
        name="post_mixer",
    )(m, x, wo, gains, w_in, w_out)


def _gmlp_kernel(x_ref, g_ref, win_ref, lng_ref, wm_ref, bs_ref, t_ref, *v_ref, nchunk, gd):
    h = _rms(x_ref[...], g_ref[0:1, :]).astype(BF16)
    u = jax.nn.gelu(_dot(h, win_ref[:, :gd]))
    vp = jax.nn.gelu(_dot(h, win_ref[:, gd:]))
    xc = vp - jnp.mean(vp, axis=-1, keepdims=True)
    v = xc * lax.rsqrt(jnp.mean(xc * xc, axis=-1, keepdims=True) + EPS) * lng_ref[...]
    if v_ref:
        v_ref[0][...] = v
    vb = v.astype(BF16)
    gw = gd // GMLP_GROUPS
    tri = (lax.broadcasted_iota(jnp.int32, (CHUNK, CHUNK), 0)
           >= lax.broadcasted_iota(jnp.int32, (CHUNK, CHUNK), 1))
    for gg in range(GMLP_GROUPS):
        wmg = jnp.where(tri, wm_ref[gg], 0.0).astype(BF16)
        bias = bs_ref[:, gg:gg + 1]
        for c in range(nchunk):
            rs = slice(c * CHUNK, (c + 1) * CHUNK)
            cs = slice(gg * gw, (gg + 1) * gw)
            mixed = _dot(wmg, vb[rs, cs]) + bias
            t_ref[rs, cs] = (u[rs, cs] * mixed).astype(BF16)


def _gmlp(x, gains, w_in, ln_g, wm, bs_t, tm, emit_v):
    rows, d = x.shape
    gd = w_in.shape[1] // 2
    out_shape = [jax.ShapeDtypeStruct((rows, gd), BF16)]
    out_specs = [pl.BlockSpec((tm, gd), lambda i: (i, 0))]
    if emit_v:
        out_shape.append(jax.ShapeDtypeStruct((rows, gd), F32))
        out_specs.append(pl.BlockSpec((tm, gd), lambda i: (i, 0)))
    return pl.pallas_call(
        functools.partial(_gmlp_kernel, nchunk=tm // CHUNK, gd=gd),
        grid=(rows // tm,),
        in_specs=[
            pl.BlockSpec((tm, d), lambda i: (i, 0)),
            _resident(gains.shape),
            _resident(w_in.shape),
            _resident(ln_g.shape),
            _resident(wm.shape),
            _resident(bs_t.shape),
        ],
        out_specs=out_specs,
        out_shape=out_shape,
        compiler_params=_params(("parallel",)),
        name="gmlp_mix",
    )(x, gains, w_in, ln_g, wm, bs_t)


def _nsa_proj_kernel(x_ref, g_ref, wq_ref, wkv_ref, wg_ref,
                     q_ref, kvc_ref, kvs_ref, kvw_ref, kvct_ref, kvst_ref, kvwt_ref, gate_ref,
                     ksel_ref, vselt_ref, kwin_ref, vwint_ref, *, tm, seq):
    h = _rms(x_ref[...], g_ref[0:1, :]).astype(BF16)
    q_ref[...] = (_dot(h, wq_ref[...]) * ATTN_SCALE).astype(BF16)
    kv = _dot(h, wkv_ref[...])
    kvc = kv[:, :KV_DIM]
    kvs = kv[:, KV_DIM:2 * KV_DIM]
    kvw = kv[:, 2 * KV_DIM:]
    kvc_ref[...] = kvc
    kvs_ref[...] = kvs
    kvw_ref[...] = kvw
    kvst = kvs.T
    kvwt = kvw.T
    kvct = kvc.T
    for p in range(tm // PAGE_SIZE):
        cols = slice(p * PAGE_SIZE, (p + 1) * PAGE_SIZE)
        kvct_ref[p] = kvct[:, cols]
        kvst_ref[p] = kvst[:, cols]
        kvwt_ref[p] = kvwt[:, cols]
    gz = _dot(h, wg_ref[...])
    for g in range(KV_GROUPS):
        gate_ref[g] = jax.nn.sigmoid(gz[:, g * LANES:(g + 1) * LANES])
    t0 = lax.rem(pl.program_id(0) * tm, seq)
    blk = (t0 + lax.broadcasted_iota(jnp.int32, (tm, LANES), 0)) // SEL_BLOCK
    onehot = jnp.where(lax.broadcasted_iota(jnp.int32, (tm, LANES), 1) == blk,
                       -MASK_BIG, 0.0).astype(BF16)
    ones = jnp.ones((HEAD_DIM, tm), BF16)
    for g in range(KV_GROUPS):
        ks = slice(g * HEAD_DIM, (g + 1) * HEAD_DIM)
        vs = slice(GKV + g * HEAD_DIM, GKV + (g + 1) * HEAD_DIM)
        ksel_ref[g, :, 0:LANES] = onehot
        ksel_ref[g, :, LANES:LANES + HEAD_DIM] = kvs[:, ks].astype(BF16)
        kwin_ref[g] = kvw[:, ks].astype(BF16)
        vselt_ref[g, 0:HEAD_DIM, :] = kvst[vs, :].astype(BF16)
        vselt_ref[g, HEAD_DIM:, :] = ones
        vwint_ref[g, 0:HEAD_DIM, :] = kvwt[vs, :].astype(BF16)
        vwint_ref[g, HEAD_DIM:, :] = ones


def _nsa_proj(x, gains, wq, wkv, wg, tm, seq):
    rows, d = x.shape
    row = lambda i: (i, 0)
    grow = lambda i: (0, i, 0)
    gcol = lambda i: (0, 0, i)
    kaug = LANES + HEAD_DIM
    npg = tm // PAGE_SIZE
    f32 = lambda *s: jax.ShapeDtypeStruct(s, F32)
    bf16 = lambda *s: jax.ShapeDtypeStruct(s, BF16)
    out_shape = [
        bf16(rows, Q_DIM),
        f32(rows, KV_DIM), f32(rows, KV_DIM), f32(rows, KV_DIM),
        f32(rows // PAGE_SIZE, KV_DIM, PAGE_SIZE), f32(rows // PAGE_SIZE, KV_DIM, PAGE_SIZE),
        f32(rows // PAGE_SIZE, KV_DIM, PAGE_SIZE),
        f32(KV_GROUPS, rows, LANES),
        bf16(KV_GROUPS, rows, kaug), bf16(KV_GROUPS, 2 * HEAD_DIM, rows),
        bf16(KV_GROUPS, rows, HEAD_DIM), bf16(KV_GROUPS, 2 * HEAD_DIM, rows),
    ]
    page = pl.BlockSpec((npg, KV_DIM, PAGE_SIZE), lambda i: (i, 0, 0))
    out_specs = [
        pl.BlockSpec((tm, Q_DIM), row),
        pl.BlockSpec((tm, KV_DIM), row), pl.BlockSpec((tm, KV_DIM), row),
        pl.BlockSpec((tm, KV_DIM), row),
        page, page, page,
        pl.BlockSpec((KV_GROUPS, tm, LANES), grow),
        pl.BlockSpec((KV_GROUPS, tm, kaug), grow),
        pl.BlockSpec((KV_GROUPS, 2 * HEAD_DIM, tm), gcol),
        pl.BlockSpec((KV_GROUPS, tm, HEAD_DIM), grow),
        pl.BlockSpec((KV_GROUPS, 2 * HEAD_DIM, tm), gcol),
    ]
    return pl.pallas_call(
        functools.partial(_nsa_proj_kernel, tm=tm, seq=seq),
        grid=(rows // tm,),
        in_specs=[
            pl.BlockSpec((tm, d), row),
            _resident(gains.shape),
            _resident(wq.shape),
            _resident(wkv.shape),
            _resident(wg.shape),
        ],
        out_specs=out_specs,
        out_shape=out_shape,
        compiler_params=_params(("parallel",)),
        name="nsa_proj",
    )(x, gains, wq, wkv, wg)


def _compress_ab_compute(xget, bd_ref, ab_ref):
    for kv in range(2):
        acc = None
        for l in range(CMP_STRIDE):
            part = _dot(xget(l, kv).astype(BF16), bd_ref[kv, l])
            acc = part if acc is None else acc + part
        ab_ref[:, kv * 2 * GKV:(kv + 1) * 2 * GKV] = acc


def _compress_ab_kernel(*refs, nrow):
    x_refs, (bd_ref, ab_ref) = refs[:-2], refs[-2:]
    per_kv = GKV // LANES

    def xget(l, kv):
        return jnp.concatenate([x_refs[kv * per_kv + h][pl.ds(l, nrow, stride=CMP_STRIDE), :]
                                for h in range(per_kv)], axis=1)

    _compress_ab_compute(xget, bd_ref, ab_ref)


def _compress_ab(x, bd, nb):
    seq = x.shape[0] // nb
    n = seq // CMP_STRIDE
    rb = min(n, 256)
    nr = n // rb
    ncol = KV_DIM // LANES
    xspecs = [pl.BlockSpec((rb * CMP_STRIDE, LANES), lambda b, r, j=j: (b * nr + r, j))
              for j in range(ncol)]
    return pl.pallas_call(
        functools.partial(_compress_ab_kernel, nrow=rb),
        grid=(nb, nr),
        in_specs=xspecs + [_resident(bd.shape)],
        out_specs=pl.BlockSpec((None, rb, 4 * GKV), lambda b, r: (b, r, 0)),
        out_shape=jax.ShapeDtypeStruct((nb, n, 4 * GKV), F32),
        compiler_params=_params(("parallel", "parallel")),
        name="compress_ab",
    )(*([x] * ncol), bd)


def _compress_ab_paged_kernel(pt_ref, pool_ref, perm_ref, bd_ref, ab_ref, xbuf, xl_ref, sem,
                              *, pgs, nsplit):
    s = pl.program_id(0)
    nsteps = pl.num_programs(0)

    def copies(step, slot):
        b = step // nsplit
        h = step % nsplit
        return [pltpu.make_async_copy(pool_ref.at[pt_ref[b, h * pgs + p]], xbuf.at[slot, p],
                                      sem.at[slot]) for p in range(pgs)]

    @pl.when(s == 0)
    def _():
        for c in copies(s, 0):
            c.start()

    @pl.when(s + 1 < nsteps)
    def _():
        for c in copies(s + 1, (s + 1) % 2):
            c.start()

    slot = s % 2
    for c in copies(s, slot):
        c.wait()

    rows_pp = PAGE_SIZE // CMP_STRIDE
    perm = perm_ref[...]

    def body(p, carry):
        r0 = pl.multiple_of(p * rows_pp, rows_pp)
        for kv in range(2):
            y = _dot_nt(perm, xbuf[slot, p, kv].astype(BF16))
            for l in range(CMP_STRIDE):
                xl_ref[l, pl.ds(r0, rows_pp), kv * GKV:(kv + 1) * GKV] = (
                    y[l * rows_pp:(l + 1) * rows_pp, :])
        return carry

    lax.fori_loop(0, pgs, body, 0, unroll=8)
    _compress_ab_compute(lambda l, kv: xl_ref[l, :, kv * GKV:(kv + 1) * GKV], bd_ref, ab_ref)


def _compress_ab_paged(pt, pool_t, bd, pgs):
    nb, n_pages = pt.shape
    nsplit = n_pages // pgs
    rows_pp = PAGE_SIZE // CMP_STRIDE
    rb = pgs * rows_pp
    ln = np.arange(PAGE_SIZE)
    perm = jnp.asarray(np.arange(PAGE_SIZE)[None, :]
                       == (CMP_STRIDE * (ln % rows_pp) + ln // rows_pp)[:, None], dtype=BF16)
    grid_spec = pltpu.PrefetchScalarGridSpec(
        num_scalar_prefetch=1,
        grid=(nb * nsplit,),
        in_specs=[pl.BlockSpec(memory_space=pl.ANY),
                  pl.BlockSpec(perm.shape, lambda s, pt: (0, 0), pipeline_mode=pl.Buffered(1)),
                  pl.BlockSpec(bd.shape, lambda s, pt: (0,) * 4, pipeline_mode=pl.Buffered(1))],
        out_specs=pl.BlockSpec((None, rb, 4 * GKV), lambda s, pt: (s // nsplit, s % nsplit, 0)),
        scratch_shapes=[pltpu.VMEM((2, pgs, 2, GKV, PAGE_SIZE), F32),
                        pltpu.VMEM((CMP_STRIDE, rb, KV_DIM), F32),
                        pltpu.SemaphoreType.DMA((2,))],
    )
    return pl.pallas_call(
        functools.partial(_compress_ab_paged_kernel, pgs=pgs, nsplit=nsplit),
        grid_spec=grid_spec,
        out_shape=jax.ShapeDtypeStruct((nb, n_pages * rows_pp, 4 * GKV), F32),
        compiler_params=_params(("arbitrary",)),
        name="compress_ab_paged",
    )(pt, pool_t, perm, bd)


def _compress_fin_kernel(ab_ref, pe_ref, w1_ref, w2_ref, kc_ref, vct_ref, *, n):
    for kv in range(2):
        a = ab_ref[:, kv * 2 * GKV:kv * 2 * GKV + GKV]
        b = ab_ref[:, kv * 2 * GKV + GKV:(kv + 1) * 2 * GKV]
        c = _dot(pe_ref[kv], w1_ref[kv])[0:1, :]
        c4 = jnp.concatenate([c] * KV_GROUPS, axis=1)
        hid = a + pltpu.roll(b, n - 1, 0) + c4
        o = _dot(jax.nn.gelu(hid).astype(BF16), w2_ref[kv])
        if kv == 0:
            for g in range(KV_GROUPS):
                kc_ref[g] = o[:, g * HEAD_DIM:(g + 1) * HEAD_DIM].astype(BF16)
        else:
            ot = o.T
            for g in range(KV_GROUPS):
                vct_ref[g] = ot[g * HEAD_DIM:(g + 1) * HEAD_DIM, :].astype(BF16)


def _compress_fin(ab, pe8, w1f, w2bd):
    nb, n, _ = ab.shape
    return pl.pallas_call(
        functools.partial(_compress_fin_kernel, n=n),
        grid=(nb,),
        in_specs=[pl.BlockSpec((None, n, 4 * GKV), lambda b: (b, 0, 0)),
                  _resident(pe8.shape), _resident(w1f.shape), _resident(w2bd.shape)],
        out_specs=[pl.BlockSpec((None, KV_GROUPS, n, HEAD_DIM), lambda b: (b, 0, 0, 0)),
                   pl.BlockSpec((None, KV_GROUPS, HEAD_DIM, n), lambda b: (b, 0, 0, 0))],
        out_shape=[jax.ShapeDtypeStruct((nb, KV_GROUPS, n, HEAD_DIM), BF16),
                   jax.ShapeDtypeStruct((nb, KV_GROUPS, HEAD_DIM, n), BF16)],
        compiler_params=_params(("parallel",)),
        name="compress_fin",
    )(ab, pe8, w1f, w2bd)


def _group_rows(qt):
    return jnp.concatenate([qt[:, r * HEAD_DIM:(r + 1) * HEAD_DIM] for r in range(GROUP_SIZE)],
                           axis=0)


def _ungroup_t(ot, tq):
    return jnp.concatenate([ot[:, r * tq:(r + 1) * tq].T for r in range(GROUP_SIZE)], axis=1)


def _lane_qpos(s0, tq):
    q = lax.broadcasted_iota(jnp.int32, (1, tq), 1)
    return s0 + jnp.concatenate([q] * GROUP_SIZE, axis=1)


def _gate_lanes(gtt, j, tq):
    return jnp.concatenate([gtt[3 * r + j:3 * r + j + 1, :] for r in range(GROUP_SIZE)], axis=1)


def _topk_mask(score, k):
    nb = score.shape[0]
    idx = lax.broadcasted_iota(jnp.int32, score.shape, 0).astype(F32)

    taken = -3e38

    def body(_, work):
        m = jnp.max(work, axis=0, keepdims=True)
        first = jnp.min(jnp.where(work == m, idx, float(nb)), axis=0, keepdims=True)
        return jnp.where(idx == first, taken, work)

    work = lax.fori_loop(0, k, body, score, unroll=True)
    return jnp.where(work < 0.5 * taken, 1.0, 0.0)


def _cmp_select_kernel(q_ref, kc_ref, vct_ref, ovt_ref, gate_ref, oc_ref, nsel_ref, imp_ref,
                       *, tq, nq, pos0, k_top, chunk):
    s0 = pos0 + pl.program_id(2) * tq
    qrows = _group_rows(q_ref[...])
    n = kc_ref.shape[0]
    qpos = _lane_qpos(s0, tq)
    gtt = gate_ref[...].T

    def attend(nr):
        st = _dot_nt(kc_ref[0:nr, :], qrows)
        end = lax.broadcasted_iota(jnp.int32, (nr, 1), 0) * CMP_STRIDE + (CMP_BLOCK - 1)
        sm = jnp.where(end <= qpos, st, NEG)
        e = jnp.exp(sm - jnp.max(sm, axis=0, keepdims=True))
        p = e * jnp.where(qpos >= CMP_BLOCK - 1, 1.0 / jnp.sum(e, axis=0, keepdims=True), 0.0)
        oct_ = _dot(vct_ref[:, 0:nr], p.astype(BF16))
        oc_ref[...] = _ungroup_t(_gate_lanes(gtt, 0, tq) * oct_, tq).astype(BF16)
        psum = p[:, 0:tq]
        for r in range(1, GROUP_SIZE):
            psum = psum + p[:, r * tq:(r + 1) * tq]
        hi = psum.astype(BF16)
        lo = (psum - hi.astype(F32)).astype(BF16)
        imp_ref[...] = _dot(ovt_ref[:, 0:nr], hi) + _dot(ovt_ref[:, 0:nr], lo)

    def chunks_needed(s_last):
        nvis = jnp.clip((s_last - (CMP_BLOCK - 1)) // CMP_STRIDE + 1, 1, n)
        return (nvis + chunk - 1) // chunk

    if nq == 1:
        nvis = min(max((pos0 + tq - CMP_BLOCK) // CMP_STRIDE + 1, 1), n)
        attend(-(-nvis // chunk) * chunk)
    else:
        nc = chunks_needed(s0 + tq - 1)
        for c in range(1, n // chunk + 1):
            pl.when(nc == c)(functools.partial(attend, c * chunk))
    imp_t = imp_ref[...]
    shape = imp_t.shape
    blk = lax.broadcasted_iota(jnp.int32, shape, 0)
    cur = (s0 + lax.broadcasted_iota(jnp.int32, shape, 1)) // SEL_BLOCK
    valid = blk <= cur
    forced = (blk == 0) | (blk > cur - N_LOCAL)
    score = jnp.where(valid, jnp.where(forced, FORCE, imp_t), NEG)
    sel = _topk_mask(score, k_top)
    nsel_t = jnp.where((sel > 0.5) & valid, 0.0, 1.0)
    nsel_ref[...] = nsel_t.T.astype(BF16)


def _cmp_select(q, kc, vct, ovt, gates, tpad, pos0, k_top):
    nb, _, n, _ = kc.shape
    tq = min(512, tpad)
    nq = tpad // tq
    nbp = ovt.shape[0]
    gq = GROUP_SIZE * HEAD_DIM
    chunk = min(128, n)
    assert n % chunk == 0
    return pl.pallas_call(
        functools.partial(_cmp_select_kernel, tq=tq, nq=nq, pos0=pos0, k_top=k_top, chunk=chunk),
        grid=(nb, KV_GROUPS, nq),
        in_specs=[
            pl.BlockSpec((tq, gq), lambda b, g, i: (b * nq + i, g)),
            pl.BlockSpec((None, None, n, HEAD_DIM), lambda b, g, i: (b, g, 0, 0)),
            pl.BlockSpec((None, None, HEAD_DIM, n), lambda b, g, i: (b, g, 0, 0)),
            pl.BlockSpec(ovt.shape, lambda b, g, i: (0, 0)),
            pl.BlockSpec((None, tq, LANES), lambda b, g, i: (g, b * nq + i, 0)),
        ],
        out_specs=[
            pl.BlockSpec((tq, gq), lambda b, g, i: (b * nq + i, g)),
            pl.BlockSpec((None, None, tq, nbp), lambda b, g, i: (b, g, i, 0)),
        ],
        out_shape=[
            jax.ShapeDtypeStruct((nb * tpad, Q_DIM), BF16),
            jax.ShapeDtypeStruct((nb, KV_GROUPS, tpad, nbp), BF16),
        ],
        scratch_shapes=[pltpu.VMEM((nbp, tq), F32)],
        compiler_params=_params(("parallel", "parallel", "parallel")),
        name="cmp_select",
    )(q, kc, vct, ovt, gates)


def _sel_win_kernel(q_ref, nsel_ref, gate_ref, oc_ref, ksel_ref, vselt_ref, kwin_ref, vwint_ref,
                    o_ref, sa_ref, sb_ref, sw_ref, m_ref, acc_ref, *, ts, nsub, tk, seq):
    s00 = pl.program_id(2) * (ts * nsub)
    cols = GROUP_SIZE * ts
    subs = []
    for u in range(nsub):
        rs = slice(u * ts, (u + 1) * ts)
        qrows = _group_rows(q_ref[rs, :])
        ns = nsel_ref[rs, :]
        qaug = jnp.concatenate([jnp.concatenate([ns] * GROUP_SIZE, axis=0), qrows], axis=1)
        subs.append((rs, s00 + u * ts, qrows, qaug, _lane_qpos(s00 + u * ts, ts),
                     gate_ref[rs, :].T))
        m_ref[u] = jnp.full((1, cols), NEG, F32)
        acc_ref[u] = jnp.zeros((2 * HEAD_DIM, cols), F32)

    def scores(ref, j):
        k0 = pl.multiple_of(j * tk, tk)
        for u in range(nsub):
            ref[u] = _dot_nt(ksel_ref[pl.ds(k0, tk), :], subs[u][3])

    def update(ref, j, causal):
        k0 = pl.multiple_of(j * tk, tk)
        for u in range(nsub):
            nk = (u + 1) * ts if causal else tk
            st = ref[u, 0:nk, :]
            if causal:
                kpos = k0 + lax.broadcasted_iota(jnp.int32, (nk, 1), 0)
                st = jnp.where(kpos <= subs[u][4], st, NEG)
            m = m_ref[u]
            mn = jnp.maximum(m, jnp.max(st, axis=0, keepdims=True))
            p = jnp.exp(st - mn).astype(BF16)
            acc_ref[u] = jnp.exp(m - mn) * acc_ref[u] + _dot(vselt_ref[:, pl.ds(k0, nk)], p)
            m_ref[u] = mn

    jd = s00 // tk
    scores(sa_ref, 0)
    nw = min(WINDOW + ts, seq)
    w0s = [pl.multiple_of(jnp.maximum(sub[1] + ts - nw, 0), ts) for sub in subs]
    for u in range(nsub):
        sw_ref[u] = _dot_nt(kwin_ref[pl.ds(w0s[u], nw), :], subs[u][2])

    def body(i, carry):
        j = 2 * i
        scores(sb_ref, j + 1)
        update(sa_ref, j, False)
        scores(sa_ref, j + 2)
        update(sb_ref, j + 1, False)
        return carry

    lax.fori_loop(0, jd // 2, body, 0)
    odd = lax.rem(jd, 2) == 1

    @pl.when(odd)
    def _():
        scores(sb_ref, jd)
        update(sa_ref, jd - 1, False)
        update(sb_ref, jd, True)

    @pl.when(jnp.logical_not(odd))
    def _():
        update(sa_ref, jd, True)

    for u, (rs, s0, qrows, _, qpos, gtt) in enumerate(subs):
        acc = acc_ref[u]
        o_t = _gate_lanes(gtt, 1, ts) * (acc[:HEAD_DIM] / acc[HEAD_DIM:HEAD_DIM + 1])
        d = qpos - (w0s[u] + lax.broadcasted_iota(jnp.int32, (nw, 1), 0))
        sw = jnp.where((d >= 0) & (d < WINDOW), sw_ref[u], NEG)
        pw = jnp.exp(sw - jnp.max(sw, axis=0, keepdims=True)).astype(BF16)
        accw = _dot(vwint_ref[:, pl.ds(w0s[u], nw)], pw)
        o_t = o_t + _gate_lanes(gtt, 2, ts) * (accw[:HEAD_DIM] / accw[HEAD_DIM:HEAD_DIM + 1])
        o_ref[rs, :] = (oc_ref[rs, :].astype(F32) + _ungroup_t(o_t, ts)).astype(BF16)


def _sel_win(q, nsel, gates, ocg, ksel, vselt, kwin, vwint, nb, seq):
    ts = 128
    tk = min(512, seq)
    nsub = tk // ts
    tq = ts * nsub
    nq = seq // tq
    gq = GROUP_SIZE * HEAD_DIM
    qspec = pl.BlockSpec((tq, gq), lambda b, g, i: (b * nq + i, g))
    krows = lambda w: pl.BlockSpec((None, seq, w), lambda b, g, i: (g, b, 0))
    vcols = pl.BlockSpec((None, 2 * HEAD_DIM, seq), lambda b, g, i: (g, 0, b))
    return pl.pallas_call(
        functools.partial(_sel_win_kernel, ts=ts, nsub=nsub, tk=tk, seq=seq),
        grid=(nb, KV_GROUPS, nq),
        in_specs=[
            qspec,
            pl.BlockSpec((None, None, tq, LANES), lambda b, g, i: (b, g, i, 0)),
            pl.BlockSpec((None, tq, LANES), lambda b, g, i: (g, b * nq + i, 0)),
            qspec,
            krows(LANES + HEAD_DIM), vcols, krows(HEAD_DIM), vcols,
        ],
        out_specs=qspec,
        out_shape=jax.ShapeDtypeStruct((nb * seq, Q_DIM), BF16),
        scratch_shapes=[pltpu.VMEM((nsub, tk, GROUP_SIZE * ts), F32),
                        pltpu.VMEM((nsub, tk, GROUP_SIZE * ts), F32),
                        pltpu.VMEM((nsub, min(WINDOW + ts, seq), GROUP_SIZE * ts), F32),
                        pltpu.VMEM((nsub, 1, GROUP_SIZE * ts), F32),
                        pltpu.VMEM((nsub, 2 * HEAD_DIM, GROUP_SIZE * ts), F32)],
        compiler_params=_params(("parallel", "parallel", "arbitrary")),
        name="sel_win_attn",
    )(q, nsel, gates, ocg, ksel, vselt, kwin, vwint)


def _sample_attn_kernel(pt_ref, pool_ref, qbd_ref, nselr_ref, eneg_ref, knew_ref, wnew_ref,
                        wbuf_ref, gate_ref, ocg_ref, o_ref,
                        kbuf, sem, m_sc, l_sc, acc_sc, *, pgs, nch, ppt, past, tn):
    b = pl.program_id(0)
    c = pl.program_id(1)
    step = b * nch + c
    nsteps = pl.num_programs(0) * nch

    def copies(st, slot):
        bb = st // nch
        cc = st % nch
        return [pltpu.make_async_copy(pool_ref.at[pt_ref[bb, cc * pgs + p]], kbuf.at[slot, p],
                                      sem.at[slot]) for p in range(pgs)]

    @pl.when(step == 0)
    def _():
        for cp in copies(step, 0):
            cp.start()

    @pl.when(step + 1 < nsteps)
    def _():
        for cp in copies(step + 1, (step + 1) % 2):
            cp.start()

    @pl.when(c == 0)
    def _():
        m_sc[...] = jnp.full(m_sc.shape, NEG, F32)
        l_sc[...] = jnp.zeros(l_sc.shape, F32)
        acc_sc[...] = jnp.zeros(acc_sc.shape, F32)

    slot = step % 2
    for cp in copies(step, slot):
        cp.wait()

    qbd = qbd_ref[...]
    nselr = nselr_ref[...]
    rows = qbd.shape[0]
    qpos = past + lax.rem(lax.broadcasted_iota(jnp.int32, (rows, 1), 0), tn)

    def online(s, pv):
        m = m_sc[...]
        mn = jnp.maximum(m, jnp.max(s, axis=-1, keepdims=True))
        p = jnp.exp(s - mn)
        alpha = jnp.exp(m - mn)
        l_sc[...] = alpha * l_sc[...] + jnp.sum(p, axis=-1, keepdims=True)
        acc_sc[...] = alpha * acc_sc[...] + pv(p.astype(BF16))
        m_sc[...] = mn

    tk = ppt * PAGE_SIZE
    for t in range(pgs // ppt):
        kt = jnp.concatenate([kbuf[slot, t * ppt + i, 0] for i in range(ppt)], axis=1).astype(BF16)
        vt = jnp.concatenate([kbuf[slot, t * ppt + i, 1] for i in range(ppt)], axis=1).astype(BF16)
        k0 = pl.multiple_of(c * (pgs * PAGE_SIZE) + t * tk, tk)
        bias = _dot_nt(nselr, eneg_ref[pl.ds(k0, tk), :])
        online(_dot(qbd, kt) + bias, lambda p, vt=vt: _dot_nt(p, vt))

    @pl.when(c == nch - 1)
    def _():
        npad = knew_ref.shape[0]
        newpos = past + lax.broadcasted_iota(jnp.int32, (1, npad), 1)
        new_ok = (newpos <= qpos) & (newpos < past + tn)
        kn = knew_ref[:, 0:GKV].astype(BF16)
        vn = knew_ref[:, GKV:].astype(BF16)
        online(jnp.where(new_ok, _dot_nt(qbd, kn), NEG), lambda p: _dot(p, vn))
        o_s = acc_sc[...] / l_sc[...]
        wbl = wbuf_ref.shape[2]
        dw = qpos - (past - wbl + lax.broadcasted_iota(jnp.int32, (1, wbl), 1))
        s1 = jnp.where((dw >= 0) & (dw < WINDOW), _dot(qbd, wbuf_ref[0].astype(BF16)), NEG)
        s2 = jnp.where(new_ok & (qpos - newpos < WINDOW),
                       _dot_nt(qbd, wnew_ref[:, 0:GKV].astype(BF16)), NEG)
        mw = jnp.maximum(jnp.max(s1, axis=-1, keepdims=True), jnp.max(s2, axis=-1, keepdims=True))
        p1 = jnp.exp(s1 - mw)
        p2 = jnp.exp(s2 - mw)
        lw = jnp.sum(p1, axis=-1, keepdims=True) + jnp.sum(p2, axis=-1, keepdims=True)
        o_w = (_dot_nt(p1.astype(BF16), wbuf_ref[1].astype(BF16))
               + _dot(p2.astype(BF16), wnew_ref[:, GKV:].astype(BF16))) / lw
        gt = gate_ref[...]
        o_ref[...] = ocg_ref[...] + gt[:, 1:2] * o_s + gt[:, 2:3] * o_w


def _sample_attn(pt, pool_t, qbd, nselr, eneg, knew, wnew, wbuf_t, wbuf_off, gates_r, ocg_r,
                 pgs, past, tn):
    nb, n_pages = pt.shape
    nch = n_pages // pgs
    rows = qbd.shape[1]
    ppt = pgs
    per_b = lambda shape: pl.BlockSpec((None,) + shape, lambda b, c, pt: (b,) + (0,) * len(shape))
    grid_spec = pltpu.PrefetchScalarGridSpec(
        num_scalar_prefetch=1,
        grid=(nb, nch),
        in_specs=[
            pl.BlockSpec(memory_space=pl.ANY),
            per_b(qbd.shape[1:]),
            per_b(nselr.shape[1:]),
            pl.BlockSpec(eneg.shape, lambda b, c, pt: (0, 0), pipeline_mode=pl.Buffered(1)),
            per_b(knew.shape[1:]),
            per_b(wnew.shape[1:]),
            pl.BlockSpec((None,) + wbuf_t.shape[1:], lambda b, c, pt: (wbuf_off + b, 0, 0, 0)),
            per_b(gates_r.shape[1:]),
            per_b(ocg_r.shape[1:]),
        ],
        out_specs=per_b((rows, GKV)),
        scratch_shapes=[
            pltpu.VMEM((2, pgs, 2, GKV, PAGE_SIZE), F32),
            pltpu.SemaphoreType.DMA((2,)),
            pltpu.VMEM((rows, 1), F32),
            pltpu.VMEM((rows, 1), F32),
            pltpu.VMEM((rows, GKV), F32),
        ],
    )
    return pl.pallas_call(
        functools.partial(_sample_attn_kernel, pgs=pgs, nch=nch, ppt=ppt, past=past, tn=tn),
        grid_spec=grid_spec,
        out_shape=jax.ShapeDtypeStruct((nb, rows, GKV), F32),
        compiler_params=_params(("arbitrary", "arbitrary")),
        name="sample_attn",
    )(pt, pool_t, qbd, nselr, eneg, knew, wnew, wbuf_t, gates_r, ocg_r)


def _overlap_t(nbp, nrow):
    cs = np.arange(nrow)[None, :] * CMP_STRIDE
    ss = np.arange(nbp)[:, None] * SEL_BLOCK
    ov = np.minimum(cs + CMP_BLOCK, ss + SEL_BLOCK) - np.maximum(cs, ss)
    return jnp.asarray(np.clip(ov, 0, None).astype(np.float32) / CMP_BLOCK, dtype=BF16)


def _block_diag_w1(w1):
    eye = jnp.eye(KV_GROUPS, dtype=w1.dtype)
    bd = jnp.einsum('gh,klde->klgdhe', eye, w1).reshape(2, CMP_BLOCK, GKV, GKV)
    return jnp.concatenate([bd[:, :CMP_STRIDE], bd[:, CMP_STRIDE:]], axis=-1).astype(BF16)


def _block_diag_w2(w2):
    eye = jnp.eye(KV_GROUPS, dtype=w2.dtype)
    return jnp.einsum('gh,kde->kgdhe', eye, w2).reshape(2, GKV, GKV).astype(BF16)


def _gate_weight(wg):
    d = wg.shape[0]
    w = wg.reshape(d, KV_GROUPS, 3 * GROUP_SIZE)
    w = jnp.pad(w, ((0, 0), (0, 0), (0, LANES - 3 * GROUP_SIZE)))
    return w.reshape(d, KV_GROUPS * LANES).astype(BF16)


def _token_minor(cache):
    nd = cache.ndim
    perm = tuple(range(nd - 4)) + (nd - 3, nd - 2, nd - 1, nd - 4)
    t = cache.transpose(perm)
    return t.reshape(t.shape[:nd - 3] + (GKV, t.shape[-1]))


def _from_pages(pages_t, lead):
    x = pages_t.reshape(lead + (2, KV_GROUPS, HEAD_DIM, PAGE_SIZE))
    nd = x.ndim
    return x.transpose(tuple(range(nd - 4)) + (nd - 1, nd - 4, nd - 3, nd - 2))


def _nsa_layer(yp, ys, a, cache_cmp_kv, cache_sel_kv, cache_win_kv, page_table, gains,
               w_nsa_in, w_cmp_hidden, w_cmp_out, cmp_pos_emb, dims):
    bsz, seq, db, tn, past = dims
    n_pool = cache_cmp_kv.shape[1]
    n_pages = page_table.shape[1]
    w_in = w_nsa_in[a]
    wq = w_in[:, :Q_DIM].astype(BF16)
    wkv = w_in[:, Q_DIM:Q_DIM + 3 * KV_DIM].astype(BF16)
    wg = _gate_weight(w_in[:, Q_DIM + 3 * KV_DIM:])
    bd1 = _block_diag_w1(w_cmp_hidden[a])
    w2bd = _block_diag_w2(w_cmp_out[a])
    w1f = w_cmp_hidden[a].reshape(2, CMP_BLOCK * HEAD_DIM, HEAD_DIM).astype(BF16)
    pe8 = jnp.broadcast_to(cmp_pos_emb[a].reshape(2, 1, CMP_BLOCK * HEAD_DIM),
                           (2, 8, CMP_BLOCK * HEAD_DIM)).astype(BF16)

    tm = min(512, seq)
    (qp, kvc_p, _, _, kvct_p, kvst_p, kvwt_p, gate_p, ksel, vselt, kwin, vwint) = _nsa_proj(
        yp, gains, wq, wkv, wg, tm, seq)
    nrow = seq // CMP_STRIDE
    kc, vct = _compress_fin(_compress_ab(kvc_p, bd1, bsz), pe8, w1f, w2bd)
    ns_p = -(-seq // SEL_BLOCK)
    assert ns_p <= LANES
    ocg, nsel = _cmp_select(qp, kc, vct, _overlap_t(LANES, nrow), gate_p, seq, 0,
                            min(SEL_TOPN, ns_p))
    op = _sel_win(qp, nsel, gate_p, ocg, ksel, vselt, kwin, vwint, bsz, seq)

    rows_s = db * tn
    qs, kvc_s, kvs_s, kvw_s, _, _, _, gate_s, _, _, _, _ = _nsa_proj(
        ys, gains, wq, wkv, wg, rows_s, rows_s)
    pt_abs = page_table + a * n_pool
    pool_c = _token_minor(cache_cmp_kv).reshape(-1, 2, GKV, PAGE_SIZE)
    pool_s = _token_minor(cache_sel_kv).reshape(-1, 2, GKV, PAGE_SIZE)
    ab_s = _compress_ab_paged(pt_abs, pool_c, bd1, min(32, n_pages))
    kc_s, vct_s = _compress_fin(ab_s, pe8, w1f, w2bd)
    nrow_s = past // CMP_STRIDE
    ns_s = -(-(past + tn) // SEL_BLOCK)
    nbp_s = -(-ns_s // LANES) * LANES
    tpad = 128
    qs_pad = jnp.pad(qs.reshape(db, tn, Q_DIM), ((0, 0), (0, tpad - tn), (0, 0)))
    gate_pad = jnp.pad(gate_s.reshape(KV_GROUPS, db, tn, LANES),
                       ((0, 0), (0, 0), (0, tpad - tn), (0, 0)))
    ocg_s, nsel_s = _cmp_select(qs_pad.reshape(db * tpad, Q_DIM), kc_s, vct_s,
                                _overlap_t(nbp_s, nrow_s),
                                gate_pad.reshape(KV_GROUPS, db * tpad, LANES), tpad, past,
                                min(SEL_TOPN, ns_s))
    eye = jnp.eye(KV_GROUPS, dtype=BF16)
    q5 = qs.reshape(db, tn, KV_GROUPS, GROUP_SIZE, HEAD_DIM)
    rows = KV_GROUPS * GROUP_SIZE * tn
    qbd = jnp.einsum('bqgrd,gh->bgrqhd', q5, eye).reshape(db, rows, GKV)
    nselr = jnp.broadcast_to(nsel_s[:, :, None, :tn, :LANES],
                             (db, KV_GROUPS, GROUP_SIZE, tn, LANES)).reshape(db, rows, LANES)
    g4 = gate_s.reshape(KV_GROUPS, db, tn, LANES)[..., :3 * GROUP_SIZE]
    g4 = g4.reshape(KV_GROUPS, db, tn, GROUP_SIZE, 3).transpose(1, 0, 3, 2, 4)
    gates_r = jnp.pad(g4.reshape(db, rows, 3), ((0, 0), (0, 0), (0, LANES - 3)))
    oc5 = ocg_s.reshape(db, tpad, KV_GROUPS, GROUP_SIZE, HEAD_DIM)[:, :tn].astype(F32)
    ocg_r = jnp.einsum('bqgrd,gh->bgrqhd', oc5, jnp.eye(KV_GROUPS, dtype=F32)).reshape(
        db, rows, GKV)
    kpos = np.arange(past)[:, None] // SEL_BLOCK
    eneg = jnp.asarray(np.where(kpos == np.arange(LANES)[None, :], -MASK_BIG, 0.0), dtype=BF16)
    npad = 128
    knew = jnp.pad(kvs_s.reshape(db, tn, KV_DIM), ((0, 0), (0, npad - tn), (0, 0)))
    wnew = jnp.pad(kvw_s.reshape(db, tn, KV_DIM), ((0, 0), (0, npad - tn), (0, 0)))
    wbl = cache_win_kv.shape[2]
    assert past >= wbl
    wbuf_t = _token_minor(cache_win_kv).reshape(-1, 2, GKV, wbl)
    o_rows = _sample_attn(pt_abs, pool_s, qbd, nselr, eneg, knew, wnew, wbuf_t, a * db,
                          gates_r, ocg_r, min(16, n_pages), past, tn)
    o6 = o_rows.reshape(db, KV_GROUPS, GROUP_SIZE, tn, KV_GROUPS, HEAD_DIM)
    o_s = jnp.einsum('bgrqhd,gh->bqgrd', o6, jnp.eye(KV_GROUPS, dtype=F32))
    o_s = o_s.reshape(rows_s, Q_DIM).astype(BF16)

    shp = (KV_GROUPS, HEAD_DIM)
    npg = seq // PAGE_SIZE
    wpg = min(WINDOW, seq) // PAGE_SIZE
    p_win = _from_pages(kvwt_p.reshape(bsz, npg, KV_DIM, PAGE_SIZE)[:, npg - wpg:], (bsz, wpg))
    kvw_s5 = kvw_s.reshape((db, tn, 2) + shp)
    win_new = jnp.concatenate([cache_win_kv[a], kvw_s5], axis=1)[:, tn:]
    caches = (_from_pages(kvct_p, (bsz, npg)), _from_pages(kvst_p, (bsz, npg)),
              p_win.reshape((bsz, wpg * PAGE_SIZE, 2) + shp),
              kvc_s.reshape((db, tn, 2) + shp), kvs_s.reshape((db, tn, 2) + shp), win_new)
    return op, o_s, caches


def kernel(x_prompt, x_sample, cache_cmp_kv, cache_sel_kv, cache_win_kv, page_table, norm_gains,
           w_nsa_in, w_cmp_hidden, w_cmp_out, cmp_pos_emb, w_nsa_out, w_gm_in, gm_norm_gain,
           w_spatial, b_spatial, w_gm_out, w_ffn_in, w_ffn_out):
    bsz, seq, d = x_prompt.shape
    db, tn, _ = x_sample.shape
    past = page_table.shape[1] * PAGE_SIZE
    depth = norm_gains.shape[0]
    assert seq % CHUNK == 0 and past % SEL_BLOCK == 0 and tn < CMP_STRIDE
    assert (db * tn) % 8 == 0 and CHUNK % tn == 0 and db * tn == CHUNK
    dims = (bsz, seq, db, tn, past)
    yp = x_prompt.reshape(bsz * seq, d)
    ys = x_sample.reshape(db * tn, d)
    tm_p = min(512, seq)
    tm_s = db * tn
    lists = [[] for _ in range(7)]
    w_fin = w_ffn_in.astype(BF16)
    w_fout = w_ffn_out.astype(BF16)
    for i in range(depth):
        gains = norm_gains[i]
        if i % 2 == 0:
            a = i // 2
            mp, ms, caches = _nsa_layer(yp, ys, a, cache_cmp_kv, cache_sel_kv, cache_win_kv,
                                        page_table, gains, w_nsa_in, w_cmp_hidden, w_cmp_out,
                                        cmp_pos_emb, dims)
            for lst, c in zip(lists[:6], caches):
                lst.append(c)
            wo = w_nsa_out[a].astype(BF16)
        else:
            bi = i // 2
            w_in = w_gm_in[bi].astype(BF16)
            ln_g = gm_norm_gain[bi].reshape(1, -1)
            ws = w_spatial[bi]
            bs = b_spatial[bi]
            mp = _gmlp(yp, gains, w_in, ln_g, ws, bs.T, tm_p, False)[0]
            eye = jnp.eye(db, dtype=ws.dtype)
            ws_s = jnp.einsum('bc,gts->gbtcs', eye, ws[:, :tn, :tn]).reshape(-1, tm_s, tm_s)
            bs_s = jnp.tile(bs[:, :tn], (1, db)).T
            ms, v_new = _gmlp(ys, gains, w_in, ln_g, ws_s, bs_s, tm_s, True)
            lists[6].append(v_new.reshape(db, tn, -1))
            wo = w_gm_out[bi].astype(BF16)
        yp = _post(mp, yp, wo, gains, w_fin, w_fout, i, tm_p)
        ys = _post(ms, ys, wo, gains, w_fin, w_fout, i, tm_s)
    return (yp.reshape(bsz, seq, d), ys.reshape(db, tn, d)) + tuple(jnp.stack(l) for l in lists)
```

```python
import functools

import jax
import jax.numpy as jnp
import numpy as np
from jax import lax
from jax.experimental import pallas as pl
from jax.experimental.pallas import tpu as pltpu

F32 = jnp.float32
BF16 = jnp.bfloat16

HEAD_DIM = 64
KV_GROUPS = 4
GROUP_SIZE = 4
N_HEADS = KV_GROUPS * GROUP_SIZE
Q_DIM = N_HEADS * HEAD_DIM
GKV = KV_GROUPS * HEAD_DIM
KV_DIM = 2 * GKV
CMP_STRIDE = 16
CMP_BLOCK = 2 * CMP_STRIDE
SEL_BLOCK = 64
SEL_TOPN = 16
N_LOCAL = 2
WINDOW = 512
PAGE_SIZE = 128
CHUNK = 128
GMLP_GROUPS = 8
ATTN_SCALE = HEAD_DIM ** -0.5
EPS = 1e-6
NEG = -1e30
FORCE = 1e4
MASK_BIG = 1e30
LANES = 128
VMEM_LIMIT = 56 * 1024 * 1024


def _params(sem):
    return pltpu.CompilerParams(dimension_semantics=sem, vmem_limit_bytes=VMEM_LIMIT)


def _resident(shape):
    nd = len(shape)
    return pl.BlockSpec(shape, lambda *_: (0,) * nd, pipeline_mode=pl.Buffered(1))


def _rms(x, g):
    return x * lax.rsqrt(jnp.mean(x * x, axis=-1, keepdims=True) + EPS) * g


def _dot(a, b):
    return jnp.dot(a, b, preferred_element_type=F32)


def _dot_nt(a, b):
    return lax.dot_general(a, b, (((1,), (1,)), ((), ())), preferred_element_type=F32)


def _post_kernel(m_ref, x_ref, wo_ref, g_ref, win_ref, wout_ref, y_ref, *, ffn, ck):
    x = x_ref[...]
    y1 = x + _rms(_dot(m_ref[...], wo_ref[...]), g_ref[1:2, :])
    h = _rms(y1, g_ref[2:3, :]).astype(BF16)
    acc = jnp.zeros_like(x)
    for c0 in range(0, ffn, ck):
        w = min(ck, ffn - c0)
        a = _dot(h, win_ref[:, c0:c0 + w])
        b = _dot(h, win_ref[:, ffn + c0:ffn + c0 + w])
        act = (jax.nn.silu(a) * b).astype(BF16)
        acc = acc + _dot(act, wout_ref[c0:c0 + w, :])
    y_ref[...] = y1 + _rms(acc, g_ref[3:4, :])


def _layer_of(stacked, layer):
    return pl.BlockSpec((None,) + stacked.shape[1:], lambda *_: (layer, 0, 0),
                        pipeline_mode=pl.Buffered(1))


def _post(m, x, wo, gains, w_in, w_out, layer, tm):
    rows, d = x.shape
    km = m.shape[1]
    ffn = w_out.shape[1]
    return pl.pallas_call(
        functools.partial(_post_kernel, ffn=ffn, ck=256),
        grid=(rows // tm,),
        in_specs=[
            pl.BlockSpec((tm, km), lambda i: (i, 0)),
            pl.BlockSpec((tm, d), lambda i: (i, 0)),
            _resident(wo.shape),
            _resident(gains.shape),
            _layer_of(w_in, layer),
            _layer_of(w_out, layer),
        ],
        out_specs=pl.BlockSpec((tm, d), lambda i: (i, 0)),
        out_shape=jax.ShapeDtypeStruct((rows, d), F32),
        compiler_params=_params(("parallel",)),
        name="post_mixer",
    )(m, x, wo, gains, w_in, w_out)


def _gmlp_kernel(x_ref, g_ref, win_ref, lng_ref, wm_ref, bs_ref, t_ref, *v_ref, nchunk, gd):
    h = _rms(x_ref[...], g_ref[0:1, :]).astype(BF16)
    u = jax.nn.gelu(_dot(h, win_ref[:, :gd]))
    vp = jax.nn.gelu(_dot(h, win_ref[:, gd:]))
    xc = vp - jnp.mean(vp, axis=-1, keepdims=True)
    v = xc * lax.rsqrt(jnp.mean(xc * xc, axis=-1, keepdims=True) + EPS) * lng_ref[...]
    if v_ref:
        v_ref[0][...] = v
    vb = v.astype(BF16)
    gw = gd // GMLP_GROUPS
    tri = (lax.broadcasted_iota(jnp.int32, (CHUNK, CHUNK), 0)
           >= lax.broadcasted_iota(jnp.int32, (CHUNK, CHUNK), 1))
    for gg in range(GMLP_GROUPS):
        wmg = jnp.where(tri, wm_ref[gg], 0.0).astype(BF16)
        bias = bs_ref[:, gg:gg + 1]
        for c in range(nchunk):
            rs = slice(c * CHUNK, (c + 1) * CHUNK)
            cs = slice(gg * gw, (gg + 1) * gw)
            mixed = _dot(wmg, vb[rs, cs]) + bias
            t_ref[rs, cs] = (u[rs, cs] * mixed).astype(BF16)


def _gmlp(x, gains, w_in, ln_g, wm, bs_t, tm, emit_v):
    rows, d = x.shape
    gd = w_in.shape[1] // 2
    out_shape = [jax.ShapeDtypeStruct((rows, gd), BF16)]
    out_specs = [pl.BlockSpec((tm, gd), lambda i: (i, 0))]
    if emit_v:
        out_shape.append(jax.ShapeDtypeStruct((rows, gd), F32))
        out_specs.append(pl.BlockSpec((tm, gd), lambda i: (i, 0)))
    return pl.pallas_call(
        functools.partial(_gmlp_kernel, nchunk=tm // CHUNK, gd=gd),
        grid=(rows // tm,),
        in_specs=[
            pl.BlockSpec((tm, d), lambda i: (i, 0)),
            _resident(gains.shape),
            _resident(w_in.shape),
            _resident(ln_g.shape),
            _resident(wm.shape),
            _resident(bs_t.shape),
        ],
        out_specs=out_specs,
        out_shape=out_shape,
        compiler_params=_params(("parallel",)),
        name="gmlp_mix",
    )(x, gains, w_in, ln_g, wm, bs_t)


def _nsa_proj_kernel(x_ref, g_ref, wq_ref, wkv_ref, wg_ref,
                     q_ref, kvc_ref, kvs_ref, kvw_ref, kvct_ref, kvst_ref, kvwt_ref, gate_ref,
                     ksel_ref, vselt_ref, kwin_ref, vwint_ref, *, tm, seq):
    h = _rms(x_ref[...], g_ref[0:1, :]).astype(BF16)
    q_ref[...] = (_dot(h, wq_ref[...]) * ATTN_SCALE).astype(BF16)
    kv = _dot(h, wkv_ref[...])
    kvc = kv[:, :KV_DIM]
    kvs = kv[:, KV_DIM:2 * KV_DIM]
    kvw = kv[:, 2 * KV_DIM:]
    kvc_ref[...] = kvc
    kvs_ref[...] = kvs
    kvw_ref[...] = kvw
    kvst = kvs.T
    kvwt = kvw.T
    kvct = kvc.T
    for p in range(tm // PAGE_SIZE):
        cols = slice(p * PAGE_SIZE, (p + 1) * PAGE_SIZE)
        kvct_ref[p] = kvct[:, cols]
        kvst_ref[p] = kvst[:, cols]
        kvwt_ref[p] = kvwt[:, cols]
    gz = _dot(h, wg_ref[...])
    for g in range(KV_GROUPS):
        gate_ref[g] = jax.nn.sigmoid(gz[:, g * LANES:(g + 1) * LANES])
    t0 = lax.rem(pl.program_id(0) * tm, seq)
    blk = (t0 + lax.broadcasted_iota(jnp.int32, (tm, LANES), 0)) // SEL_BLOCK
    onehot = jnp.where(lax.broadcasted_iota(jnp.int32, (tm, LANES), 1) == blk,
                       -MASK_BIG, 0.0).astype(BF16)
    ones = jnp.ones((HEAD_DIM, tm), BF16)
    for g in range(KV_GROUPS):
        ks = slice(g * HEAD_DIM, (g + 1) * HEAD_DIM)
        vs = slice(GKV + g * HEAD_DIM, GKV + (g + 1) * HEAD_DIM)
        ksel_ref[g, :, 0:LANES] = onehot
        ksel_ref[g, :, LANES:LANES + HEAD_DIM] = kvs[:, ks].astype(BF16)
        kwin_ref[g] = kvw[:, ks].astype(BF16)
        vselt_ref[g, 0:HEAD_DIM, :] = kvst[vs, :].astype(BF16)
        vselt_ref[g, HEAD_DIM:, :] = ones
        vwint_ref[g, 0:HEAD_DIM, :] = kvwt[vs, :].astype(BF16)
        vwint_ref[g, HEAD_DIM:, :] = ones


def _nsa_proj(x, gains, wq, wkv, wg, tm, seq):
    rows, d = x.shape
    row = lambda i: (i, 0)
    grow = lambda i: (0, i, 0)
    gcol = lambda i: (0, 0, i)
    kaug = LANES + HEAD_DIM
    npg = tm // PAGE_SIZE
    f32 = lambda *s: jax.ShapeDtypeStruct(s, F32)
    bf16 = lambda *s: jax.ShapeDtypeStruct(s, BF16)
    out_shape = [
        bf16(rows, Q_DIM),
        f32(rows, KV_DIM), f32(rows, KV_DIM), f32(rows, KV_DIM),
        f32(rows // PAGE_SIZE, KV_DIM, PAGE_SIZE), f32(rows // PAGE_SIZE, KV_DIM, PAGE_SIZE),
        f32(rows // PAGE_SIZE, KV_DIM, PAGE_SIZE),
        f32(KV_GROUPS, rows, LANES),
        bf16(KV_GROUPS, rows, kaug), bf16(KV_GROUPS, 2 * HEAD_DIM, rows),
        bf16(KV_GROUPS, rows, HEAD_DIM), bf16(KV_GROUPS, 2 * HEAD_DIM, rows),
    ]
    page = pl.BlockSpec((npg, KV_DIM, PAGE_SIZE), lambda i: (i, 0, 0))
    out_specs = [
        pl.BlockSpec((tm, Q_DIM), row),
        pl.BlockSpec((tm, KV_DIM), row), pl.BlockSpec((tm, KV_DIM), row),
        pl.BlockSpec((tm, KV_DIM), row),
        page, page, page,
        pl.BlockSpec((KV_GROUPS, tm, LANES), grow),
        pl.BlockSpec((KV_GROUPS, tm, kaug), grow),
        pl.BlockSpec((KV_GROUPS, 2 * HEAD_DIM, tm), gcol),
        pl.BlockSpec((KV_GROUPS, tm, HEAD_DIM), grow),
        pl.BlockSpec((KV_GROUPS, 2 * HEAD_DIM, tm), gcol),
    ]
    return pl.pallas_call(
        functools.partial(_nsa_proj_kernel, tm=tm, seq=seq),
        grid=(rows // tm,),
        in_specs=[
            pl.BlockSpec((tm, d), row),
            _resident(gains.shape),
            _resident(wq.shape),
            _resident(wkv.shape),
            _resident(wg.shape),
        ],
        out_specs=out_specs,
        out_shape=out_shape,
        compiler_params=_params(("parallel",)),
        name="nsa_proj",
    )(x, gains, wq, wkv, wg)


def _compress_ab_compute(xget, bd_ref, ab_ref):
    for kv in range(2):
        acc = None
        for l in range(CMP_STRIDE):
            part = _dot(xget(l, kv).astype(BF16), bd_ref[kv, l])
            acc = part if acc is None else acc + part
        ab_ref[:, kv * 2 * GKV:(kv + 1) * 2 * GKV] = acc


def _compress_ab_kernel(*refs, nrow):
    x_refs, (bd_ref, ab_ref) = refs[:-2], refs[-2:]
    per_kv = GKV // LANES

    def xget(l, kv):
        return jnp.concatenate([x_refs[kv * per_kv + h][pl.ds(l, nrow, stride=CMP_STRIDE), :]
                                for h in range(per_kv)], axis=1)

    _compress_ab_compute(xget, bd_ref, ab_ref)


def _compress_ab(x, bd, nb):
    seq = x.shape[0] // nb
    n = seq // CMP_STRIDE
    rb = min(n, 256)
    nr = n // rb
    ncol = KV_DIM // LANES
    xspecs = [pl.BlockSpec((rb * CMP_STRIDE, LANES), lambda b, r, j=j: (b * nr + r, j))
              for j in range(ncol)]
    return pl.pallas_call(
        functools.partial(_compress_ab_kernel, nrow=rb),
        grid=(nb, nr),
        in_specs=xspecs + [_resident(bd.shape)],
        out_specs=pl.BlockSpec((None, rb, 4 * GKV), lambda b, r: (b, r, 0)),
        out_shape=jax.ShapeDtypeStruct((nb, n, 4 * GKV), F32),
        compiler_params=_params(("parallel", "parallel")),
        name="compress_ab",
    )(*([x] * ncol), bd)


def _compress_ab_paged_kernel(pt_ref, pool_ref, perm_ref, bd_ref, ab_ref, xbuf, xl_ref, sem,
                              *, pgs, nsplit):
    s = pl.program_id(0)
    nsteps = pl.num_programs(0)

    def copies(step, slot):
        b = step // nsplit
        h = step % nsplit
        return [pltpu.make_async_copy(pool_ref.at[pt_ref[b, h * pgs + p]], xbuf.at[slot, p],
                                      sem.at[slot]) for p in range(pgs)]

    @pl.when(s == 0)
    def _():
        for c in copies(s, 0):
            c.start()

    @pl.when(s + 1 < nsteps)
    def _():
        for c in copies(s + 1, (s + 1) % 2):
            c.start()

    slot = s % 2
    for c in copies(s, slot):
        c.wait()

    rows_pp = PAGE_SIZE // CMP_STRIDE
    perm = perm_ref[...]

    def body(p, carry):
        r0 = pl.multiple_of(p * rows_pp, rows_pp)
        for kv in range(2):
            y = _dot_nt(perm, xbuf[slot, p, kv].astype(BF16))
            for l in range(CMP_STRIDE):
                xl_ref[l, pl.ds(r0, rows_pp), kv * GKV:(kv + 1) * GKV] = (
                    y[l * rows_pp:(l + 1) * rows_pp, :])
        return carry

    lax.fori_loop(0, pgs, body, 0, unroll=8)
    _compress_ab_compute(lambda l, kv: xl_ref[l, :, kv * GKV:(kv + 1) * GKV], bd_ref, ab_ref)


def _compress_ab_paged(pt, pool_t, bd, pgs):
    nb, n_pages = pt.shape
    nsplit = n_pages // pgs
    rows_pp = PAGE_SIZE // CMP_STRIDE
    rb = pgs * rows_pp
    ln = np.arange(PAGE_SIZE)
    perm = jnp.asarray(np.arange(PAGE_SIZE)[None, :]
                       == (CMP_STRIDE * (ln % rows_pp) + ln // rows_pp)[:, None], dtype=BF16)
    grid_spec = pltpu.PrefetchScalarGridSpec(
        num_scalar_prefetch=1,
        grid=(nb * nsplit,),
        in_specs=[pl.BlockSpec(memory_space=pl.ANY),
                  pl.BlockSpec(perm.shape, lambda s, pt: (0, 0), pipeline_mode=pl.Buffered(1)),
                  pl.BlockSpec(bd.shape, lambda s, pt: (0,) * 4, pipeline_mode=pl.Buffered(1))],
        out_specs=pl.BlockSpec((None, rb, 4 * GKV), lambda s, pt: (s // nsplit, s % nsplit, 0)),
        scratch_shapes=[pltpu.VMEM((2, pgs, 2, GKV, PAGE_SIZE), F32),
                        pltpu.VMEM((CMP_STRIDE, rb, KV_DIM), F32),
                        pltpu.SemaphoreType.DMA((2,))],
    )
    return pl.pallas_call(
        functools.partial(_compress_ab_paged_kernel, pgs=pgs, nsplit=nsplit),
        grid_spec=grid_spec,
        out_shape=jax.ShapeDtypeStruct((nb, n_pages * rows_pp, 4 * GKV), F32),
        compiler_params=_params(("arbitrary",)),
        name="compress_ab_paged",
    )(pt, pool_t, perm, bd)


def _compress_fin_kernel(ab_ref, pe_ref, w1_ref, w2_ref, kc_ref, vct_ref, *, n):
    for kv in range(2):
        a = ab_ref[:, kv * 2 * GKV:kv * 2 * GKV + GKV]
        b = ab_ref[:, kv * 2 * GKV + GKV:(kv + 1) * 2 * GKV]
        c = _dot(pe_ref[kv], w1_ref[kv])[0:1, :]
        c4 = jnp.concatenate([c] * KV_GROUPS, axis=1)
        hid = a + pltpu.roll(b, n - 1, 0) + c4
        o = _dot(jax.nn.gelu(hid).astype(BF16), w2_ref[kv])
        if kv == 0:
            for g in range(KV_GROUPS):
                kc_ref[g] = o[:, g * HEAD_DIM:(g + 1) * HEAD_DIM].astype(BF16)
        else:
            ot = o.T
            for g in range(KV_GROUPS):
                vct_ref[g] = ot[g * HEAD_DIM:(g + 1) * HEAD_DIM, :].astype(BF16)


def _compress_fin(ab, pe8, w1f, w2bd):
    nb, n, _ = ab.shape
    return pl.pallas_call(
        functools.partial(_compress_fin_kernel, n=n),
        grid=(nb,),
        in_specs=[pl.BlockSpec((None, n, 4 * GKV), lambda b: (b, 0, 0)),
                  _resident(pe8.shape), _resident(w1f.shape), _resident(w2bd.shape)],
        out_specs=[pl.BlockSpec((None, KV_GROUPS, n, HEAD_DIM), lambda b: (b, 0, 0, 0)),
                   pl.BlockSpec((None, KV_GROUPS, HEAD_DIM, n), lambda b: (b, 0, 0, 0))],
        out_shape=[jax.ShapeDtypeStruct((nb, KV_GROUPS, n, HEAD_DIM), BF16),
                   jax.ShapeDtypeStruct((nb, KV_GROUPS, HEAD_DIM, n), BF16)],
        compiler_params=_params(("parallel",)),
        name="compress_fin",
    )(ab, pe8, w1f, w2bd)


def _group_rows(qt):
    return jnp.concatenate([qt[:, r * HEAD_DIM:(r + 1) * HEAD_DIM] for r in range(GROUP_SIZE)],
                           axis=0)


def _ungroup_t(ot, tq):
    return jnp.concatenate([ot[:, r * tq:(r + 1) * tq].T for r in range(GROUP_SIZE)], axis=1)


def _lane_qpos(s0, tq):
    q = lax.broadcasted_iota(jnp.int32, (1, tq), 1)
    return s0 + jnp.concatenate([q] * GROUP_SIZE, axis=1)


def _gate_lanes(gtt, j, tq):
    return jnp.concatenate([gtt[3 * r + j:3 * r + j + 1, :] for r in range(GROUP_SIZE)], axis=1)


def _topk_mask(score, k):
    nb = score.shape[0]
    idx = lax.broadcasted_iota(jnp.int32, score.shape, 0).astype(F32)

    taken = -3e38

    def body(_, work):
        m = jnp.max(work, axis=0, keepdims=True)
        first = jnp.min(jnp.where(work == m, idx, float(nb)), axis=0, keepdims=True)
        return jnp.where(idx == first, taken, work)

    work = lax.fori_loop(0, k, body, score, unroll=True)
    return jnp.where(work < 0.5 * taken, 1.0, 0.0)


def _cmp_select_kernel(q_ref, kc_ref, vct_ref, ovt_ref, gate_ref, oc_ref, nsel_ref, imp_ref,
                       *, tq, nq, pos0, k_top, chunk, nblk):
    s0 = pos0 + pl.program_id(2) * tq
    qrows = _group_rows(q_ref[...])
    n = kc_ref.shape[0]
    qpos = _lane_qpos(s0, tq)
    gtt = gate_ref[...].T

    def attend(nr):
        st = _dot_nt(kc_ref[0:nr, :], qrows)
        end = lax.broadcasted_iota(jnp.int32, (nr, 1), 0) * CMP_STRIDE + (CMP_BLOCK - 1)
        sm = jnp.where(end <= qpos, st, NEG)
        e = jnp.exp(sm - jnp.max(sm, axis=0, keepdims=True))
        p = e * jnp.where(qpos >= CMP_BLOCK - 1, 1.0 / jnp.sum(e, axis=0, keepdims=True), 0.0)
        oct_ = _dot(vct_ref[:, 0:nr], p.astype(BF16))
        oc_ref[...] = _ungroup_t(_gate_lanes(gtt, 0, tq) * oct_, tq).astype(BF16)
        psum = p[:, 0:tq]
        for r in range(1, GROUP_SIZE):
            psum = psum + p[:, r * tq:(r + 1) * tq]
        hi = psum.astype(BF16)
        lo = (psum - hi.astype(F32)).astype(BF16)
        imp_ref[...] = _dot(ovt_ref[:, 0:nr], hi) + _dot(ovt_ref[:, 0:nr], lo)

    def chunks_needed(s_last):
        nvis = jnp.clip((s_last - (CMP_BLOCK - 1)) // CMP_STRIDE + 1, 1, n)
        return (nvis + chunk - 1) // chunk

    if nq == 1:
        nvis = min(max((pos0 + tq - CMP_BLOCK) // CMP_STRIDE + 1, 1), n)
        attend(-(-nvis // chunk) * chunk)
    else:
        nc = chunks_needed(s0 + tq - 1)
        for c in range(1, n // chunk + 1):
            pl.when(nc == c)(functools.partial(attend, c * chunk))
    imp_t = imp_ref[...]
    shape = imp_t.shape
    blk = lax.broadcasted_iota(jnp.int32, shape, 0)
    cur = (s0 + lax.broadcasted_iota(jnp.int32, shape, 1)) // SEL_BLOCK
    valid = (blk <= cur) & (blk < nblk)
    forced = (blk == 0) | (blk > cur - N_LOCAL)
    score = jnp.where(valid, jnp.where(forced, FORCE, imp_t), NEG)
    sel = _topk_mask(score, k_top)
    nsel_t = jnp.where((sel > 0.5) & valid, 0.0, 1.0)
    nsel_ref[...] = nsel_t.T.astype(BF16)


def _cmp_select(q, kc, vct, ovt, gates, tpad, pos0, k_top, nblk):
    nb, _, n, _ = kc.shape
    tq = min(512, tpad)
    nq = tpad // tq
    nbp = ovt.shape[0]
    gq = GROUP_SIZE * HEAD_DIM
    chunk = min(128, n)
    assert n % chunk == 0
    return pl.pallas_call(
        functools.partial(_cmp_select_kernel, tq=tq, nq=nq, pos0=pos0, k_top=k_top, chunk=chunk,
                          nblk=nblk),
        grid=(nb, KV_GROUPS, nq),
        in_specs=[
            pl.BlockSpec((tq, gq), lambda b, g, i: (b * nq + i, g)),
            pl.BlockSpec((None, None, n, HEAD_DIM), lambda b, g, i: (b, g, 0, 0)),
            pl.BlockSpec((None, None, HEAD_DIM, n), lambda b, g, i: (b, g, 0, 0)),
            pl.BlockSpec(ovt.shape, lambda b, g, i: (0, 0)),
            pl.BlockSpec((None, tq, LANES), lambda b, g, i: (g, b * nq + i, 0)),
        ],
        out_specs=[
            pl.BlockSpec((tq, gq), lambda b, g, i: (b * nq + i, g)),
            pl.BlockSpec((None, None, tq, nbp), lambda b, g, i: (b, g, i, 0)),
        ],
        out_shape=[
            jax.ShapeDtypeStruct((nb * tpad, Q_DIM), BF16),
            jax.ShapeDtypeStruct((nb, KV_GROUPS, tpad, nbp), BF16),
        ],
        scratch_shapes=[pltpu.VMEM((nbp, tq), F32)],
        compiler_params=_params(("parallel", "parallel", "parallel")),
        name="cmp_select",
    )(q, kc, vct, ovt, gates)


def _sel_win_kernel(q_ref, nsel_ref, gate_ref, oc_ref, ksel_ref, vselt_ref, kwin_ref, vwint_ref,
                    o_ref, sa_ref, sb_ref, sw_ref, m_ref, acc_ref, ow_ref, *, ts, nsub, tk, seq):
    s00 = pl.program_id(2) * (ts * nsub)
    cols = GROUP_SIZE * ts
    subs = []
    for u in range(nsub):
        rs = slice(u * ts, (u + 1) * ts)
        qrows = _group_rows(q_ref[rs, :])
        ns = nsel_ref[rs, :]
        qaug = jnp.concatenate([jnp.concatenate([ns] * GROUP_SIZE, axis=0), qrows], axis=1)
        subs.append((rs, s00 + u * ts, qrows, qaug, _lane_qpos(s00 + u * ts, ts),
                     gate_ref[rs, :].T))
        m_ref[u] = jnp.full((1, cols), NEG, F32)
        acc_ref[u] = jnp.zeros((2 * HEAD_DIM, cols), F32)

    def scores(ref, j):
        k0 = pl.multiple_of(j * tk, tk)
        for u in range(nsub):
            ref[u] = _dot_nt(ksel_ref[pl.ds(k0, tk), :], subs[u][3])

    def update(ref, j, causal):
        k0 = pl.multiple_of(j * tk, tk)
        ps = []
        for u in range(nsub):
            nk = (u + 1) * ts if causal else tk
            st = ref[u, 0:nk, :]
            if causal:
                kpos = k0 + lax.broadcasted_iota(jnp.int32, (nk, 1), 0)
                st = jnp.where(kpos <= subs[u][4], st, NEG)
            m = m_ref[u]
            mn = jnp.maximum(m, jnp.max(st, axis=0, keepdims=True))
            ps.append((nk, jnp.exp(st - mn).astype(BF16), jnp.exp(m - mn)))
            m_ref[u] = mn
        for u, (nk, p, alpha) in enumerate(ps):
            acc_ref[u] = alpha * acc_ref[u] + _dot(vselt_ref[:, pl.ds(k0, nk)], p)

    jd = s00 // tk
    nw = min(WINDOW + ts, seq)
    w0s = [pl.multiple_of(jnp.maximum(sub[1] + ts - nw, 0), ts) for sub in subs]
    for u in range(nsub):
        sw_ref[u] = _dot_nt(kwin_ref[pl.ds(w0s[u], nw), :], subs[u][2])
    scores(sa_ref, 0)
    for u, (rs, s0, qrows, _, qpos, gtt) in enumerate(subs):
        d = qpos - (w0s[u] + lax.broadcasted_iota(jnp.int32, (nw, 1), 0))
        sw = jnp.where((d >= 0) & (d < WINDOW), sw_ref[u], NEG)
        pw = jnp.exp(sw - jnp.max(sw, axis=0, keepdims=True)).astype(BF16)
        accw = _dot(vwint_ref[:, pl.ds(w0s[u], nw)], pw)
        ow_ref[u] = _gate_lanes(gtt, 2, ts) * (accw[:HEAD_DIM] / accw[HEAD_DIM:HEAD_DIM + 1])

    def body(i, carry):
        j = 2 * i
        scores(sb_ref, j + 1)
        update(sa_ref, j, False)
        scores(sa_ref, j + 2)
        update(sb_ref, j + 1, False)
        return carry

    lax.fori_loop(0, jd // 2, body, 0)
    odd = lax.rem(jd, 2) == 1

    @pl.when(odd)
    def _():
        scores(sb_ref, jd)
        update(sa_ref, jd - 1, False)
        update(sb_ref, jd, True)

    @pl.when(jnp.logical_not(odd))
    def _():
        update(sa_ref, jd, True)

    for u, (rs, s0, qrows, _, qpos, gtt) in enumerate(subs):
        acc = acc_ref[u]
        o_t = ow_ref[u] + _gate_lanes(gtt, 1, ts) * (acc[:HEAD_DIM] / acc[HEAD_DIM:HEAD_DIM + 1])
        o_ref[rs, :] = (oc_ref[rs, :].astype(F32) + _ungroup_t(o_t, ts)).astype(BF16)


def _sel_win(q, nsel, gates, ocg, ksel, vselt, kwin, vwint, nb, seq):
    ts = 128
    tk = min(512, seq)
    nsub = tk // ts
    tq = ts * nsub
    nq = seq // tq
    gq = GROUP_SIZE * HEAD_DIM
    qspec = pl.BlockSpec((tq, gq), lambda b, g, i: (b * nq + i, g))
    krows = lambda w: pl.BlockSpec((None, seq, w), lambda b, g, i: (g, b, 0))
    vcols = pl.BlockSpec((None, 2 * HEAD_DIM, seq), lambda b, g, i: (g, 0, b))
    return pl.pallas_call(
        functools.partial(_sel_win_kernel, ts=ts, nsub=nsub, tk=tk, seq=seq),
        grid=(nb, KV_GROUPS, nq),
        in_specs=[
            qspec,
            pl.BlockSpec((None, None, tq, LANES), lambda b, g, i: (b, g, i, 0)),
            pl.BlockSpec((None, tq, LANES), lambda b, g, i: (g, b * nq + i, 0)),
            qspec,
            krows(LANES + HEAD_DIM), vcols, krows(HEAD_DIM), vcols,
        ],
        out_specs=qspec,
        out_shape=jax.ShapeDtypeStruct((nb * seq, Q_DIM), BF16),
        scratch_shapes=[pltpu.VMEM((nsub, tk, GROUP_SIZE * ts), F32),
                        pltpu.VMEM((nsub, tk, GROUP_SIZE * ts), F32),
                        pltpu.VMEM((nsub, min(WINDOW + ts, seq), GROUP_SIZE * ts), F32),
                        pltpu.VMEM((nsub, 1, GROUP_SIZE * ts), F32),
                        pltpu.VMEM((nsub, 2 * HEAD_DIM, GROUP_SIZE * ts), F32),
                        pltpu.VMEM((nsub, HEAD_DIM, GROUP_SIZE * ts), F32)],
        compiler_params=_params(("parallel", "parallel", "arbitrary")),
        name="sel_win_attn",
    )(q, nsel, gates, ocg, ksel, vselt, kwin, vwint)


def _sample_attn_kernel(pt_ref, pool_ref, qbd_ref, nselr_ref, eneg_ref, knew_ref, wnew_ref,
                        wbuf_ref, gate_ref, ocg_ref, o_ref,
                        kbuf, sem, m_sc, l_sc, acc_sc, *, pgs, nch, ppt, past, tn):
    b = pl.program_id(0)
    c = pl.program_id(1)
    step = b * nch + c
    nsteps = pl.num_programs(0) * nch

    def copies(st, slot):
        bb = st // nch
        cc = st % nch
        return [pltpu.make_async_copy(pool_ref.at[pt_ref[bb, cc * pgs + p]], kbuf.at[slot, p],
                                      sem.at[slot]) for p in range(pgs)]

    @pl.when(step == 0)
    def _():
        for cp in copies(step, 0):
            cp.start()

    @pl.when(step + 1 < nsteps)
    def _():
        for cp in copies(step + 1, (step + 1) % 2):
            cp.start()

    @pl.when(c == 0)
    def _():
        m_sc[...] = jnp.full(m_sc.shape, NEG, F32)
        l_sc[...] = jnp.zeros(l_sc.shape, F32)
        acc_sc[...] = jnp.zeros(acc_sc.shape, F32)

    slot = step % 2
    for cp in copies(step, slot):
        cp.wait()

    qbd = qbd_ref[...]
    nselr = nselr_ref[...]
    rows = qbd.shape[0]
    qpos = past + lax.rem(lax.broadcasted_iota(jnp.int32, (rows, 1), 0), tn)

    def online(s, pv):
        m = m_sc[...]
        mn = jnp.maximum(m, jnp.max(s, axis=-1, keepdims=True))
        p = jnp.exp(s - mn)
        alpha = jnp.exp(m - mn)
        l_sc[...] = alpha * l_sc[...] + jnp.sum(p, axis=-1, keepdims=True)
        acc_sc[...] = alpha * acc_sc[...] + pv(p.astype(BF16))
        m_sc[...] = mn

    tk = ppt * PAGE_SIZE
    for t in range(pgs // ppt):
        kt = jnp.concatenate([kbuf[slot, t * ppt + i, 0] for i in range(ppt)], axis=1).astype(BF16)
        vt = jnp.concatenate([kbuf[slot, t * ppt + i, 1] for i in range(ppt)], axis=1).astype(BF16)
        k0 = pl.multiple_of(c * (pgs * PAGE_SIZE) + t * tk, tk)
        bias = _dot_nt(nselr, eneg_ref[pl.ds(k0, tk), :])
        online(_dot(qbd, kt) + bias, lambda p, vt=vt: _dot_nt(p, vt))

    @pl.when(c == nch - 1)
    def _():
        npad = knew_ref.shape[0]
        newpos = past + lax.broadcasted_iota(jnp.int32, (1, npad), 1)
        new_ok = (newpos <= qpos) & (newpos < past + tn)
        kn = knew_ref[:, 0:GKV].astype(BF16)
        vn = knew_ref[:, GKV:].astype(BF16)
        online(jnp.where(new_ok, _dot_nt(qbd, kn), NEG), lambda p: _dot(p, vn))
        o_s = acc_sc[...] / l_sc[...]
        wbl = wbuf_ref.shape[2]
        dw = qpos - (past - wbl + lax.broadcasted_iota(jnp.int32, (1, wbl), 1))
        s1 = jnp.where((dw >= 0) & (dw < WINDOW), _dot(qbd, wbuf_ref[0].astype(BF16)), NEG)
        s2 = jnp.where(new_ok & (qpos - newpos < WINDOW),
                       _dot_nt(qbd, wnew_ref[:, 0:GKV].astype(BF16)), NEG)
        mw = jnp.maximum(jnp.max(s1, axis=-1, keepdims=True), jnp.max(s2, axis=-1, keepdims=True))
        p1 = jnp.exp(s1 - mw)
        p2 = jnp.exp(s2 - mw)
        lw = jnp.sum(p1, axis=-1, keepdims=True) + jnp.sum(p2, axis=-1, keepdims=True)
        o_w = (_dot_nt(p1.astype(BF16), wbuf_ref[1].astype(BF16))
               + _dot(p2.astype(BF16), wnew_ref[:, GKV:].astype(BF16))) / lw
        gt = gate_ref[...]
        o_ref[...] = ocg_ref[...] + gt[:, 1:2] * o_s + gt[:, 2:3] * o_w


def _sample_attn(pt, pool_t, qbd, nselr, eneg, knew, wnew, wbuf_t, wbuf_off, gates_r, ocg_r,
                 pgs, past, tn):
    nb, n_pages = pt.shape
    nch = n_pages // pgs
    rows = qbd.shape[1]
    ppt = pgs
    per_b = lambda shape: pl.BlockSpec((None,) + shape, lambda b, c, pt: (b,) + (0,) * len(shape))
    grid_spec = pltpu.PrefetchScalarGridSpec(
        num_scalar_prefetch=1,
        grid=(nb, nch),
        in_specs=[
            pl.BlockSpec(memory_space=pl.ANY),
            per_b(qbd.shape[1:]),
            per_b(nselr.shape[1:]),
            pl.BlockSpec(eneg.shape, lambda b, c, pt: (0, 0), pipeline_mode=pl.Buffered(1)),
            per_b(knew.shape[1:]),
            per_b(wnew.shape[1:]),
            pl.BlockSpec((None,) + wbuf_t.shape[1:], lambda b, c, pt: (wbuf_off + b, 0, 0, 0)),
            per_b(gates_r.shape[1:]),
            per_b(ocg_r.shape[1:]),
        ],
        out_specs=per_b((rows, GKV)),
        scratch_shapes=[
            pltpu.VMEM((2, pgs, 2, GKV, PAGE_SIZE), F32),
            pltpu.SemaphoreType.DMA((2,)),
            pltpu.VMEM((rows, 1), F32),
            pltpu.VMEM((rows, 1), F32),
            pltpu.VMEM((rows, GKV), F32),
        ],
    )
    return pl.pallas_call(
        functools.partial(_sample_attn_kernel, pgs=pgs, nch=nch, ppt=ppt, past=past, tn=tn),
        grid_spec=grid_spec,
        out_shape=jax.ShapeDtypeStruct((nb, rows, GKV), F32),
        compiler_params=_params(("arbitrary", "arbitrary")),
        name="sample_attn",
    )(pt, pool_t, qbd, nselr, eneg, knew, wnew, wbuf_t, gates_r, ocg_r)


def _overlap_t(nbp, nrow):
    cs = np.arange(nrow)[None, :] * CMP_STRIDE
    ss = np.arange(nbp)[:, None] * SEL_BLOCK
    ov = np.minimum(cs + CMP_BLOCK, ss + SEL_BLOCK) - np.maximum(cs, ss)
    return jnp.asarray(np.clip(ov, 0, None).astype(np.float32) / CMP_BLOCK, dtype=BF16)


def _block_diag_w1(w1):
    eye = jnp.eye(KV_GROUPS, dtype=w1.dtype)
    bd = jnp.einsum('gh,klde->klgdhe', eye, w1).reshape(2, CMP_BLOCK, GKV, GKV)
    return jnp.concatenate([bd[:, :CMP_STRIDE], bd[:, CMP_STRIDE:]], axis=-1).astype(BF16)


def _block_diag_w2(w2):
    eye = jnp.eye(KV_GROUPS, dtype=w2.dtype)
    return jnp.einsum('gh,kde->kgdhe', eye, w2).reshape(2, GKV, GKV).astype(BF16)


def _gate_weight(wg):
    d = wg.shape[0]
    w = wg.reshape(d, KV_GROUPS, 3 * GROUP_SIZE)
    w = jnp.pad(w, ((0, 0), (0, 0), (0, LANES - 3 * GROUP_SIZE)))
    return w.reshape(d, KV_GROUPS * LANES).astype(BF16)


def _token_minor(cache):
    nd = cache.ndim
    perm = tuple(range(nd - 4)) + (nd - 3, nd - 2, nd - 1, nd - 4)
    t = cache.transpose(perm)
    return t.reshape(t.shape[:nd - 3] + (GKV, t.shape[-1]))


def _from_pages(pages_t, lead):
    x = pages_t.reshape(lead + (2, KV_GROUPS, HEAD_DIM, PAGE_SIZE))
    nd = x.ndim
    return x.transpose(tuple(range(nd - 4)) + (nd - 1, nd - 4, nd - 3, nd - 2))


def _nsa_layer(yp, ys, a, cache_cmp_kv, cache_sel_kv, cache_win_kv, page_table, gains,
               w_nsa_in, w_cmp_hidden, w_cmp_out, cmp_pos_emb, dims):
    bsz, seq, db, tn, past = dims
    n_pool = cache_cmp_kv.shape[1]
    n_pages = page_table.shape[1]
    w_in = w_nsa_in[a]
    wq = w_in[:, :Q_DIM].astype(BF16)
    wkv = w_in[:, Q_DIM:Q_DIM + 3 * KV_DIM].astype(BF16)
    wg = _gate_weight(w_in[:, Q_DIM + 3 * KV_DIM:])
    bd1 = _block_diag_w1(w_cmp_hidden[a])
    w2bd = _block_diag_w2(w_cmp_out[a])
    w1f = w_cmp_hidden[a].reshape(2, CMP_BLOCK * HEAD_DIM, HEAD_DIM).astype(BF16)
    pe8 = jnp.broadcast_to(cmp_pos_emb[a].reshape(2, 1, CMP_BLOCK * HEAD_DIM),
                           (2, 8, CMP_BLOCK * HEAD_DIM)).astype(BF16)

    tm = min(512, seq)
    (qp, kvc_p, _, _, kvct_p, kvst_p, kvwt_p, gate_p, ksel, vselt, kwin, vwint) = _nsa_proj(
        yp, gains, wq, wkv, wg, tm, seq)
    nrow = seq // CMP_STRIDE
    kc, vct = _compress_fin(_compress_ab(kvc_p, bd1, bsz), pe8, w1f, w2bd)
    ns_p = -(-seq // SEL_BLOCK)
    assert ns_p <= LANES
    ocg, nsel = _cmp_select(qp, kc, vct, _overlap_t(LANES, nrow), gate_p, seq, 0,
                            min(SEL_TOPN, ns_p), ns_p)
    op = _sel_win(qp, nsel, gate_p, ocg, ksel, vselt, kwin, vwint, bsz, seq)

    rows_s = db * tn
    qs, kvc_s, kvs_s, kvw_s, _, _, _, gate_s, _, _, _, _ = _nsa_proj(
        ys, gains, wq, wkv, wg, rows_s, rows_s)
    pt_abs = page_table + a * n_pool
    pool_c = _token_minor(cache_cmp_kv).reshape(-1, 2, GKV, PAGE_SIZE)
    pool_s = _token_minor(cache_sel_kv).reshape(-1, 2, GKV, PAGE_SIZE)
    ab_s = _compress_ab_paged(pt_abs, pool_c, bd1, min(32, n_pages))
    kc_s, vct_s = _compress_fin(ab_s, pe8, w1f, w2bd)
    nrow_s = past // CMP_STRIDE
    assert tn <= SEL_BLOCK and past // SEL_BLOCK <= LANES
    ns_s = past // SEL_BLOCK
    nbp_s = -(-ns_s // LANES) * LANES
    tpad = 128
    qs_pad = jnp.pad(qs.reshape(db, tn, Q_DIM), ((0, 0), (0, tpad - tn), (0, 0)))
    gate_pad = jnp.pad(gate_s.reshape(KV_GROUPS, db, tn, LANES),
                       ((0, 0), (0, 0), (0, tpad - tn), (0, 0)))
    ocg_s, nsel_s = _cmp_select(qs_pad.reshape(db * tpad, Q_DIM), kc_s, vct_s,
                                _overlap_t(nbp_s, nrow_s),
                                gate_pad.reshape(KV_GROUPS, db * tpad, LANES), tpad, past,
                                min(SEL_TOPN, ns_s + 1) - 1, ns_s)
    eye = jnp.eye(KV_GROUPS, dtype=BF16)
    q5 = qs.reshape(db, tn, KV_GROUPS, GROUP_SIZE, HEAD_DIM)
    rows = KV_GROUPS * GROUP_SIZE * tn
    qbd = jnp.einsum('bqgrd,gh->bgrqhd', q5, eye).reshape(db, rows, GKV)
    nselr = jnp.broadcast_to(nsel_s[:, :, None, :tn, :LANES],
                             (db, KV_GROUPS, GROUP_SIZE, tn, LANES)).reshape(db, rows, LANES)
    g4 = gate_s.reshape(KV_GROUPS, db, tn, LANES)[..., :3 * GROUP_SIZE]
    g4 = g4.reshape(KV_GROUPS, db, tn, GROUP_SIZE, 3).transpose(1, 0, 3, 2, 4)
    gates_r = jnp.pad(g4.reshape(db, rows, 3), ((0, 0), (0, 0), (0, LANES - 3)))
    oc5 = ocg_s.reshape(db, tpad, KV_GROUPS, GROUP_SIZE, HEAD_DIM)[:, :tn].astype(F32)
    ocg_r = jnp.einsum('bqgrd,gh->bgrqhd', oc5, jnp.eye(KV_GROUPS, dtype=F32)).reshape(
        db, rows, GKV)
    kpos = np.arange(past)[:, None] // SEL_BLOCK
    eneg = jnp.asarray(np.where(kpos == np.arange(LANES)[None, :], -MASK_BIG, 0.0), dtype=BF16)
    npad = 128
    knew = jnp.pad(kvs_s.reshape(db, tn, KV_DIM), ((0, 0), (0, npad - tn), (0, 0)))
    wnew = jnp.pad(kvw_s.reshape(db, tn, KV_DIM), ((0, 0), (0, npad - tn), (0, 0)))
    wbl = cache_win_kv.shape[2]
    assert past >= wbl
    wbuf_t = _token_minor(cache_win_kv).reshape(-1, 2, GKV, wbl)
    o_rows = _sample_attn(pt_abs, pool_s, qbd, nselr, eneg, knew, wnew, wbuf_t, a * db,
                          gates_r, ocg_r, min(16, n_pages), past, tn)
    o6 = o_rows.reshape(db, KV_GROUPS, GROUP_SIZE, tn, KV_GROUPS, HEAD_DIM)
    o_s = jnp.einsum('bgrqhd,gh->bqgrd', o6, jnp.eye(KV_GROUPS, dtype=F32))
    o_s = o_s.reshape(rows_s, Q_DIM).astype(BF16)

    shp = (KV_GROUPS, HEAD_DIM)
    npg = seq // PAGE_SIZE
    wpg = min(WINDOW, seq) // PAGE_SIZE
    p_win = _from_pages(kvwt_p.reshape(bsz, npg, KV_DIM, PAGE_SIZE)[:, npg - wpg:], (bsz, wpg))
    kvw_s5 = kvw_s.reshape((db, tn, 2) + shp)
    win_new = jnp.concatenate([cache_win_kv[a], kvw_s5], axis=1)[:, tn:]
    caches = (_from_pages(kvct_p, (bsz, npg)), _from_pages(kvst_p, (bsz, npg)),
              p_win.reshape((bsz, wpg * PAGE_SIZE, 2) + shp),
              kvc_s.reshape((db, tn, 2) + shp), kvs_s.reshape((db, tn, 2) + shp), win_new)
    return op, o_s, caches


def kernel(x_prompt, x_sample, cache_cmp_kv, cache_sel_kv, cache_win_kv, page_table, norm_gains,
           w_nsa_in, w_cmp_hidden, w_cmp_out, cmp_pos_emb, w_nsa_out, w_gm_in, gm_norm_gain,
           w_spatial, b_spatial, w_gm_out, w_ffn_in, w_ffn_out):
    bsz, seq, d = x_prompt.shape
    db, tn, _ = x_sample.shape
    past = page_table.shape[1] * PAGE_SIZE
    depth = norm_gains.shape[0]
    assert seq % CHUNK == 0 and past % SEL_BLOCK == 0 and tn < CMP_STRIDE
    assert (db * tn) % 8 == 0 and CHUNK % tn == 0 and db * tn == CHUNK
    dims = (bsz, seq, db, tn, past)
    yp = x_prompt.reshape(bsz * seq, d)
    ys = x_sample.reshape(db * tn, d)
    tm_p = min(512, seq)
    tm_s = db * tn
    lists = [[] for _ in range(7)]
    w_fin = w_ffn_in.astype(BF16)
    w_fout = w_ffn_out.astype(BF16)
    for i in range(depth):
        gains = norm_gains[i]
        if i % 2 == 0:
            a = i // 2
            mp, ms, caches = _nsa_layer(yp, ys, a, cache_cmp_kv, cache_sel_kv, cache_win_kv,
                                        page_table, gains, w_nsa_in, w_cmp_hidden, w_cmp_out,
                                        cmp_pos_emb, dims)
            for lst, c in zip(lists[:6], caches):
                lst.append(c)
            wo = w_nsa_out[a].astype(BF16)
        else:
            bi = i // 2
            w_in = w_gm_in[bi].astype(BF16)
            ln_g = gm_norm_gain[bi].reshape(1, -1)
            ws = w_spatial[bi]
            bs = b_spatial[bi]
            mp = _gmlp(yp, gains, w_in, ln_g, ws, bs.T, tm_p, False)[0]
            eye = jnp.eye(db, dtype=ws.dtype)
            ws_s = jnp.einsum('bc,gts->gbtcs', eye, ws[:, :tn, :tn]).reshape(-1, tm_s, tm_s)
            bs_s = jnp.tile(bs[:, :tn], (1, db)).T
            ms, v_new = _gmlp(ys, gains, w_in, ln_g, ws_s, bs_s, tm_s, True)
            lists[6].append(v_new.reshape(db, tn, -1))
            wo = w_gm_out[bi].astype(BF16)
        yp = _post(mp, yp, wo, gains, w_fin, w_fout, i, tm_p)
        ys = _post(ms, ys, wo, gains, w_fin, w_fout, i, tm_s)
    return (yp.reshape(bsz, seq, d), ys.reshape(db, tn, d)) + tuple(jnp.stack(l) for l in lists)
```

```python
import functools

import jax
import jax.numpy as jnp
import numpy as np
from jax import lax
from jax.experimental import pallas as pl
from jax.experimental.pallas import tpu as pltpu

F32 = jnp.float32
BF16 = jnp.bfloat16

HEAD_DIM = 64
KV_GROUPS = 4
GROUP_SIZE = 4
N_HEADS = KV_GROUPS * GROUP_SIZE
Q_DIM = N_HEADS * HEAD_DIM
GKV = KV_GROUPS * HEAD_DIM
KV_DIM = 2 * GKV
CMP_STRIDE = 16
CMP_BLOCK = 2 * CMP_STRIDE
SEL_BLOCK = 64
SEL_TOPN = 16
N_LOCAL = 2
WINDOW = 512
PAGE_SIZE = 128
CHUNK = 128
GMLP_GROUPS = 8
ATTN_SCALE = HEAD_DIM ** -0.5
EPS = 1e-6
NEG = -1e30
FORCE = 1e4
MASK_BIG = 1e30
LANES = 128
VMEM_LIMIT = 56 * 1024 * 1024


def _params(sem):
    return pltpu.CompilerParams(dimension_semantics=sem, vmem_limit_bytes=VMEM_LIMIT)


def _resident(shape):
    nd = len(shape)
    return pl.BlockSpec(shape, lambda *_: (0,) * nd, pipeline_mode=pl.Buffered(1))


def _rms(x, g):
    return x * lax.rsqrt(jnp.mean(x * x, axis=-1, keepdims=True) + EPS) * g


def _dot(a, b):
    return jnp.dot(a, b, preferred_element_type=F32)


def _dot_nt(a, b):
    return lax.dot_general(a, b, (((1,), (1,)), ((), ())), preferred_element_type=F32)


def _post_kernel(m_ref, x_ref, wo_ref, g_ref, win_ref, wout_ref, y_ref, *, ffn, ck):
    x = x_ref[...]
    y1 = x + _rms(_dot(m_ref[...], wo_ref[...]), g_ref[1:2, :])
    h = _rms(y1, g_ref[2:3, :]).astype(BF16)
    acc = jnp.zeros_like(x)
    for c0 in range(0, ffn, ck):
        w = min(ck, ffn - c0)
        a = _dot(h, win_ref[:, c0:c0 + w])
        b = _dot(h, win_ref[:, ffn + c0:ffn + c0 + w])
        act = (jax.nn.silu(a) * b).astype(BF16)
        acc = acc + _dot(act, wout_ref[c0:c0 + w, :])
    y_ref[...] = y1 + _rms(acc, g_ref[3:4, :])


def _layer_of(stacked, layer):
    return pl.BlockSpec((None,) + stacked.shape[1:], lambda *_: (layer, 0, 0),
                        pipeline_mode=pl.Buffered(1))


def _post(m, x, wo, gains, w_in, w_out, layer, tm):
    rows, d = x.shape
    km = m.shape[1]
    ffn = w_out.shape[1]
    return pl.pallas_call(
        functools.partial(_post_kernel, ffn=ffn, ck=256),
        grid=(rows // tm,),
        in_specs=[
            pl.BlockSpec((tm, km), lambda i: (i, 0)),
            pl.BlockSpec((tm, d), lambda i: (i, 0)),
            _resident(wo.shape),
            _resident(gains.shape),
            _layer_of(w_in, layer),
            _layer_of(w_out, layer),
        ],
        out_specs=pl.BlockSpec((tm, d), lambda i: (i, 0)),
        out_shape=jax.ShapeDtypeStruct((rows, d), F32),
        compiler_params=_params(("parallel",)),
        name="post_mixer",
    )(m, x, wo, gains, w_in, w_out)


def _gmlp_kernel(x_ref, g_ref, win_ref, lng_ref, wm_ref, bs_ref, t_ref, *v_ref, nchunk, gd):
    h = _rms(x_ref[...], g_ref[0:1, :]).astype(BF16)
    u = jax.nn.gelu(_dot(h, win_ref[:, :gd]))
    vp = jax.nn.gelu(_dot(h, win_ref[:, gd:]))
    xc = vp - jnp.mean(vp, axis=-1, keepdims=True)
    v = xc * lax.rsqrt(jnp.mean(xc * xc, axis=-1, keepdims=True) + EPS) * lng_ref[...]
    if v_ref:
        v_ref[0][...] = v
    vb = v.astype(BF16)
    gw = gd // GMLP_GROUPS
    tri = (lax.broadcasted_iota(jnp.int32, (CHUNK, CHUNK), 0)
           >= lax.broadcasted_iota(jnp.int32, (CHUNK, CHUNK), 1))
    for gg in range(GMLP_GROUPS):
        wmg = jnp.where(tri, wm_ref[gg], 0.0).astype(BF16)
        bias = bs_ref[:, gg:gg + 1]
        for c in range(nchunk):
            rs = slice(c * CHUNK, (c + 1) * CHUNK)
            cs = slice(gg * gw, (gg + 1) * gw)
            mixed = _dot(wmg, vb[rs, cs]) + bias
            t_ref[rs, cs] = (u[rs, cs] * mixed).astype(BF16)


def _gmlp(x, gains, w_in, ln_g, wm, bs_t, tm, emit_v):
    rows, d = x.shape
    gd = w_in.shape[1] // 2
    out_shape = [jax.ShapeDtypeStruct((rows, gd), BF16)]
    out_specs = [pl.BlockSpec((tm, gd), lambda i: (i, 0))]
    if emit_v:
        out_shape.append(jax.ShapeDtypeStruct((rows, gd), F32))
        out_specs.append(pl.BlockSpec((tm, gd), lambda i: (i, 0)))
    return pl.pallas_call(
        functools.partial(_gmlp_kernel, nchunk=tm // CHUNK, gd=gd),
        grid=(rows // tm,),
        in_specs=[
            pl.BlockSpec((tm, d), lambda i: (i, 0)),
            _resident(gains.shape),
            _resident(w_in.shape),
            _resident(ln_g.shape),
            _resident(wm.shape),
            _resident(bs_t.shape),
        ],
        out_specs=out_specs,
        out_shape=out_shape,
        compiler_params=_params(("parallel",)),
        name="gmlp_mix",
    )(x, gains, w_in, ln_g, wm, bs_t)


def _nsa_proj_kernel(x_ref, g_ref, wq_ref, wkv_ref, wg_ref,
                     q_ref, kvc_ref, kvs_ref, kvw_ref, kvct_ref, kvst_ref, kvwt_ref, gate_ref,
                     ksel_ref, vselt_ref, kwin_ref, vwint_ref, *, tm, seq):
    h = _rms(x_ref[...], g_ref[0:1, :]).astype(BF16)
    q_ref[...] = (_dot(h, wq_ref[...]) * ATTN_SCALE).astype(BF16)
    kv = _dot(h, wkv_ref[...])
    kvc = kv[:, :KV_DIM]
    kvs = kv[:, KV_DIM:2 * KV_DIM]
    kvw = kv[:, 2 * KV_DIM:]
    kvc_ref[...] = kvc
    kvs_ref[...] = kvs
    kvw_ref[...] = kvw
    kvst = kvs.T
    kvwt = kvw.T
    kvct = kvc.T
    for p in range(tm // PAGE_SIZE):
        cols = slice(p * PAGE_SIZE, (p + 1) * PAGE_SIZE)
        kvct_ref[p] = kvct[:, cols]
        kvst_ref[p] = kvst[:, cols]
        kvwt_ref[p] = kvwt[:, cols]
    gz = _dot(h, wg_ref[...])
    for g in range(KV_GROUPS):
        gate_ref[g] = jax.nn.sigmoid(gz[:, g * LANES:(g + 1) * LANES])
    t0 = lax.rem(pl.program_id(0) * tm, seq)
    blk = (t0 + lax.broadcasted_iota(jnp.int32, (tm, LANES), 0)) // SEL_BLOCK
    onehot = jnp.where(lax.broadcasted_iota(jnp.int32, (tm, LANES), 1) == blk,
                       -MASK_BIG, 0.0).astype(BF16)
    ones = jnp.ones((HEAD_DIM, tm), BF16)
    for g in range(KV_GROUPS):
        ks = slice(g * HEAD_DIM, (g + 1) * HEAD_DIM)
        vs = slice(GKV + g * HEAD_DIM, GKV + (g + 1) * HEAD_DIM)
        ksel_ref[g, :, 0:LANES] = onehot
        ksel_ref[g, :, LANES:LANES + HEAD_DIM] = kvs[:, ks].astype(BF16)
        kwin_ref[g] = kvw[:, ks].astype(BF16)
        vselt_ref[g, 0:HEAD_DIM, :] = kvst[vs, :].astype(BF16)
        vselt_ref[g, HEAD_DIM:, :] = ones
        vwint_ref[g, 0:HEAD_DIM, :] = kvwt[vs, :].astype(BF16)
        vwint_ref[g, HEAD_DIM:, :] = ones


def _nsa_proj(x, gains, wq, wkv, wg, tm, seq):
    rows, d = x.shape
    row = lambda i: (i, 0)
    grow = lambda i: (0, i, 0)
    gcol = lambda i: (0, 0, i)
    kaug = LANES + HEAD_DIM
    npg = tm // PAGE_SIZE
    f32 = lambda *s: jax.ShapeDtypeStruct(s, F32)
    bf16 = lambda *s: jax.ShapeDtypeStruct(s, BF16)
    out_shape = [
        bf16(rows, Q_DIM),
        f32(rows, KV_DIM), f32(rows, KV_DIM), f32(rows, KV_DIM),
        f32(rows // PAGE_SIZE, KV_DIM, PAGE_SIZE), f32(rows // PAGE_SIZE, KV_DIM, PAGE_SIZE),
        f32(rows // PAGE_SIZE, KV_DIM, PAGE_SIZE),
        f32(KV_GROUPS, rows, LANES),
        bf16(KV_GROUPS, rows, kaug), bf16(KV_GROUPS, 2 * HEAD_DIM, rows),
        bf16(KV_GROUPS, rows, HEAD_DIM), bf16(KV_GROUPS, 2 * HEAD_DIM, rows),
    ]
    page = pl.BlockSpec((npg, KV_DIM, PAGE_SIZE), lambda i: (i, 0, 0))
    out_specs = [
        pl.BlockSpec((tm, Q_DIM), row),
        pl.BlockSpec((tm, KV_DIM), row), pl.BlockSpec((tm, KV_DIM), row),
        pl.BlockSpec((tm, KV_DIM), row),
        page, page, page,
        pl.BlockSpec((KV_GROUPS, tm, LANES), grow),
        pl.BlockSpec((KV_GROUPS, tm, kaug), grow),
        pl.BlockSpec((KV_GROUPS, 2 * HEAD_DIM, tm), gcol),
        pl.BlockSpec((KV_GROUPS, tm, HEAD_DIM), grow),
        pl.BlockSpec((KV_GROUPS, 2 * HEAD_DIM, tm), gcol),
    ]
    return pl.pallas_call(
        functools.partial(_nsa_proj_kernel, tm=tm, seq=seq),
        grid=(rows // tm,),
        in_specs=[
            pl.BlockSpec((tm, d), row),
            _resident(gains.shape),
            _resident(wq.shape),
            _resident(wkv.shape),
            _resident(wg.shape),
        ],
        out_specs=out_specs,
        out_shape=out_shape,
        compiler_params=_params(("parallel",)),
        name="nsa_proj",
    )(x, gains, wq, wkv, wg)


def _compress_ab_compute(xget, bd_ref, ab_ref):
    for kv in range(2):
        acc = None
        for l in range(CMP_STRIDE):
            part = _dot(xget(l, kv).astype(BF16), bd_ref[kv, l])
            acc = part if acc is None else acc + part
        ab_ref[:, kv * 2 * GKV:(kv + 1) * 2 * GKV] = acc


def _compress_ab_kernel(*refs, nrow):
    x_refs, (bd_ref, ab_ref) = refs[:-2], refs[-2:]
    per_kv = GKV // LANES

    def xget(l, kv):
        return jnp.concatenate([x_refs[kv * per_kv + h][pl.ds(l, nrow, stride=CMP_STRIDE), :]
                                for h in range(per_kv)], axis=1)

    _compress_ab_compute(xget, bd_ref, ab_ref)


def _compress_ab(x, bd, nb):
    seq = x.shape[0] // nb
    n = seq // CMP_STRIDE
    rb = min(n, 256)
    nr = n // rb
    ncol = KV_DIM // LANES
    xspecs = [pl.BlockSpec((rb * CMP_STRIDE, LANES), lambda b, r, j=j: (b * nr + r, j))
              for j in range(ncol)]
    return pl.pallas_call(
        functools.partial(_compress_ab_kernel, nrow=rb),
        grid=(nb, nr),
        in_specs=xspecs + [_resident(bd.shape)],
        out_specs=pl.BlockSpec((None, rb, 4 * GKV), lambda b, r: (b, r, 0)),
        out_shape=jax.ShapeDtypeStruct((nb, n, 4 * GKV), F32),
        compiler_params=_params(("parallel", "parallel")),
        name="compress_ab",
    )(*([x] * ncol), bd)


def _compress_ab_paged_kernel(pt_ref, pool_ref, perm_ref, bd_ref, ab_ref, xbuf, xl_ref, sem,
                              *, pgs, nsplit):
    s = pl.program_id(0)
    nsteps = pl.num_programs(0)

    def copies(step, slot):
        b = step // nsplit
        h = step % nsplit
        return [pltpu.make_async_copy(pool_ref.at[pt_ref[b, h * pgs + p]], xbuf.at[slot, p],
                                      sem.at[slot]) for p in range(pgs)]

    @pl.when(s == 0)
    def _():
        for c in copies(s, 0):
            c.start()

    @pl.when(s + 1 < nsteps)
    def _():
        for c in copies(s + 1, (s + 1) % 2):
            c.start()

    slot = s % 2
    for c in copies(s, slot):
        c.wait()

    rows_pp = PAGE_SIZE // CMP_STRIDE
    perm = perm_ref[...]

    def body(p, carry):
        r0 = pl.multiple_of(p * rows_pp, rows_pp)
        for kv in range(2):
            y = _dot_nt(perm, xbuf[slot, p, kv].astype(BF16))
            for l in range(CMP_STRIDE):
                xl_ref[l, pl.ds(r0, rows_pp), kv * GKV:(kv + 1) * GKV] = (
                    y[l * rows_pp:(l + 1) * rows_pp, :])
        return carry

    lax.fori_loop(0, pgs, body, 0, unroll=16)
    _compress_ab_compute(lambda l, kv: xl_ref[l, :, kv * GKV:(kv + 1) * GKV], bd_ref, ab_ref)


def _compress_ab_paged(pt, pool_t, bd, pgs):
    nb, n_pages = pt.shape
    nsplit = n_pages // pgs
    rows_pp = PAGE_SIZE // CMP_STRIDE
    rb = pgs * rows_pp
    ln = np.arange(PAGE_SIZE)
    perm = jnp.asarray(np.arange(PAGE_SIZE)[None, :]
                       == (CMP_STRIDE * (ln % rows_pp) + ln // rows_pp)[:, None], dtype=BF16)
    grid_spec = pltpu.PrefetchScalarGridSpec(
        num_scalar_prefetch=1,
        grid=(nb * nsplit,),
        in_specs=[pl.BlockSpec(memory_space=pl.ANY),
                  pl.BlockSpec(perm.shape, lambda s, pt: (0, 0), pipeline_mode=pl.Buffered(1)),
                  pl.BlockSpec(bd.shape, lambda s, pt: (0,) * 4, pipeline_mode=pl.Buffered(1))],
        out_specs=pl.BlockSpec((None, rb, 4 * GKV), lambda s, pt: (s // nsplit, s % nsplit, 0)),
        scratch_shapes=[pltpu.VMEM((2, pgs, 2, GKV, PAGE_SIZE), F32),
                        pltpu.VMEM((CMP_STRIDE, rb, KV_DIM), F32),
                        pltpu.SemaphoreType.DMA((2,))],
    )
    return pl.pallas_call(
        functools.partial(_compress_ab_paged_kernel, pgs=pgs, nsplit=nsplit),
        grid_spec=grid_spec,
        out_shape=jax.ShapeDtypeStruct((nb, n_pages * rows_pp, 4 * GKV), F32),
        compiler_params=_params(("arbitrary",)),
        name="compress_ab_paged",
    )(pt, pool_t, perm, bd)


def _compress_fin_kernel(ab_ref, pe_ref, w1_ref, w2_ref, kc_ref, vct_ref, *, n):
    for kv in range(2):
        a = ab_ref[:, kv * 2 * GKV:kv * 2 * GKV + GKV]
        b = ab_ref[:, kv * 2 * GKV + GKV:(kv + 1) * 2 * GKV]
        c = _dot(pe_ref[kv], w1_ref[kv])[0:1, :]
        c4 = jnp.concatenate([c] * KV_GROUPS, axis=1)
        hid = a + pltpu.roll(b, n - 1, 0) + c4
        o = _dot(jax.nn.gelu(hid).astype(BF16), w2_ref[kv])
        if kv == 0:
            for g in range(KV_GROUPS):
                kc_ref[g] = o[:, g * HEAD_DIM:(g + 1) * HEAD_DIM].astype(BF16)
        else:
            ot = o.T
            for g in range(KV_GROUPS):
                vct_ref[g] = ot[g * HEAD_DIM:(g + 1) * HEAD_DIM, :].astype(BF16)


def _compress_fin(ab, pe8, w1f, w2bd):
    nb, n, _ = ab.shape
    return pl.pallas_call(
        functools.partial(_compress_fin_kernel, n=n),
        grid=(nb,),
        in_specs=[pl.BlockSpec((None, n, 4 * GKV), lambda b: (b, 0, 0)),
                  _resident(pe8.shape), _resident(w1f.shape), _resident(w2bd.shape)],
        out_specs=[pl.BlockSpec((None, KV_GROUPS, n, HEAD_DIM), lambda b: (b, 0, 0, 0)),
                   pl.BlockSpec((None, KV_GROUPS, HEAD_DIM, n), lambda b: (b, 0, 0, 0))],
        out_shape=[jax.ShapeDtypeStruct((nb, KV_GROUPS, n, HEAD_DIM), BF16),
                   jax.ShapeDtypeStruct((nb, KV_GROUPS, HEAD_DIM, n), BF16)],
        compiler_params=_params(("parallel",)),
        name="compress_fin",
    )(ab, pe8, w1f, w2bd)


def _group_rows(qt):
    return jnp.concatenate([qt[:, r * HEAD_DIM:(r + 1) * HEAD_DIM] for r in range(GROUP_SIZE)],
                           axis=0)


def _ungroup_t(ot, tq):
    return jnp.concatenate([ot[:, r * tq:(r + 1) * tq].T for r in range(GROUP_SIZE)], axis=1)


def _lane_qpos(s0, tq):
    q = lax.broadcasted_iota(jnp.int32, (1, tq), 1)
    return s0 + jnp.concatenate([q] * GROUP_SIZE, axis=1)


def _gate_lanes(gtt, j, tq):
    return jnp.concatenate([gtt[3 * r + j:3 * r + j + 1, :] for r in range(GROUP_SIZE)], axis=1)


def _topk_mask(score, k):
    nb = score.shape[0]
    idx = lax.broadcasted_iota(jnp.int32, score.shape, 0).astype(F32)

    taken = -3e38

    def body(_, work):
        m = jnp.max(work, axis=0, keepdims=True)
        first = jnp.min(jnp.where(work == m, idx, float(nb)), axis=0, keepdims=True)
        return jnp.where(idx == first, taken, work)

    work = lax.fori_loop(0, k, body, score, unroll=True)
    return jnp.where(work < 0.5 * taken, 1.0, 0.0)


def _cmp_select_kernel(q_ref, kc_ref, vct_ref, ovt_ref, gate_ref, oc_ref, nsel_ref, imp_ref,
                       *, tq, nq, pos0, k_top, chunk, nblk):
    s0 = pos0 + pl.program_id(2) * tq
    qrows = _group_rows(q_ref[...])
    n = kc_ref.shape[0]
    qpos = _lane_qpos(s0, tq)
    gtt = gate_ref[...].T

    def attend(nr):
        st = _dot_nt(kc_ref[0:nr, :], qrows)
        end = lax.broadcasted_iota(jnp.int32, (nr, 1), 0) * CMP_STRIDE + (CMP_BLOCK - 1)
        sm = jnp.where(end <= qpos, st, NEG)
        e = jnp.exp(sm - jnp.max(sm, axis=0, keepdims=True))
        p = e * jnp.where(qpos >= CMP_BLOCK - 1, 1.0 / jnp.sum(e, axis=0, keepdims=True), 0.0)
        oct_ = _dot(vct_ref[:, 0:nr], p.astype(BF16))
        oc_ref[...] = _ungroup_t(_gate_lanes(gtt, 0, tq) * oct_, tq).astype(BF16)
        psum = p[:, 0:tq]
        for r in range(1, GROUP_SIZE):
            psum = psum + p[:, r * tq:(r + 1) * tq]
        hi = psum.astype(BF16)
        lo = (psum - hi.astype(F32)).astype(BF16)
        imp_ref[...] = _dot(ovt_ref[:, 0:nr], hi) + _dot(ovt_ref[:, 0:nr], lo)

    def chunks_needed(s_last):
        nvis = jnp.clip((s_last - (CMP_BLOCK - 1)) // CMP_STRIDE + 1, 1, n)
        return (nvis + chunk - 1) // chunk

    if nq == 1:
        nvis = min(max((pos0 + tq - CMP_BLOCK) // CMP_STRIDE + 1, 1), n)
        attend(-(-nvis // chunk) * chunk)
    else:
        nc = chunks_needed(s0 + tq - 1)
        for c in range(1, n // chunk + 1):
            pl.when(nc == c)(functools.partial(attend, c * chunk))
    imp_t = imp_ref[...]
    shape = imp_t.shape
    blk = lax.broadcasted_iota(jnp.int32, shape, 0)
    cur = (s0 + lax.broadcasted_iota(jnp.int32, shape, 1)) // SEL_BLOCK
    valid = (blk <= cur) & (blk < nblk)
    forced = (blk == 0) | (blk > cur - N_LOCAL)
    score = jnp.where(valid, jnp.where(forced, FORCE, imp_t), NEG)
    sel = _topk_mask(score, k_top)
    nsel_t = jnp.where((sel > 0.5) & valid, 0.0, 1.0)
    nsel_ref[...] = nsel_t.T.astype(BF16)


def _cmp_select(q, kc, vct, ovt, gates, tpad, pos0, k_top, nblk):
    nb, _, n, _ = kc.shape
    tq = min(512, tpad)
    nq = tpad // tq
    nbp = ovt.shape[0]
    gq = GROUP_SIZE * HEAD_DIM
    chunk = min(128, n)
    assert n % chunk == 0
    return pl.pallas_call(
        functools.partial(_cmp_select_kernel, tq=tq, nq=nq, pos0=pos0, k_top=k_top, chunk=chunk,
                          nblk=nblk),
        grid=(nb, KV_GROUPS, nq),
        in_specs=[
            pl.BlockSpec((tq, gq), lambda b, g, i: (b * nq + i, g)),
            pl.BlockSpec((None, None, n, HEAD_DIM), lambda b, g, i: (b, g, 0, 0)),
            pl.BlockSpec((None, None, HEAD_DIM, n), lambda b, g, i: (b, g, 0, 0)),
            pl.BlockSpec(ovt.shape, lambda b, g, i: (0, 0)),
            pl.BlockSpec((None, tq, LANES), lambda b, g, i: (g, b * nq + i, 0)),
        ],
        out_specs=[
            pl.BlockSpec((tq, gq), lambda b, g, i: (b * nq + i, g)),
            pl.BlockSpec((None, None, tq, nbp), lambda b, g, i: (b, g, i, 0)),
        ],
        out_shape=[
            jax.ShapeDtypeStruct((nb * tpad, Q_DIM), BF16),
            jax.ShapeDtypeStruct((nb, KV_GROUPS, tpad, nbp), BF16),
        ],
        scratch_shapes=[pltpu.VMEM((nbp, tq), F32)],
        compiler_params=_params(("parallel", "parallel", "parallel")),
        name="cmp_select",
    )(q, kc, vct, ovt, gates)


def _sel_win_kernel(q_ref, nsel_ref, gate_ref, oc_ref, ksel_ref, vselt_ref, kwin_ref, vwint_ref,
                    o_ref, sa_ref, sb_ref, sw_ref, m_ref, acc_ref, ow_ref, *, ts, nsub, tk, seq):
    s00 = pl.program_id(2) * (ts * nsub)
    cols = GROUP_SIZE * ts
    subs = []
    for u in range(nsub):
        rs = slice(u * ts, (u + 1) * ts)
        qrows = _group_rows(q_ref[rs, :])
        ns = nsel_ref[rs, :]
        qaug = jnp.concatenate([jnp.concatenate([ns] * GROUP_SIZE, axis=0), qrows], axis=1)
        subs.append((rs, s00 + u * ts, qrows, qaug, _lane_qpos(s00 + u * ts, ts),
                     gate_ref[rs, :].T))
        m_ref[u] = jnp.full((1, cols), NEG, F32)
        acc_ref[u] = jnp.zeros((2 * HEAD_DIM, cols), F32)

    def scores(ref, j):
        k0 = pl.multiple_of(j * tk, tk)
        for u in range(nsub):
            ref[u] = _dot_nt(ksel_ref[pl.ds(k0, tk), :], subs[u][3])

    def update(ref, j, causal):
        k0 = pl.multiple_of(j * tk, tk)
        ps = []
        for u in range(nsub):
            nk = (u + 1) * ts if causal else tk
            st = ref[u, 0:nk, :]
            if causal:
                kpos = k0 + lax.broadcasted_iota(jnp.int32, (nk, 1), 0)
                st = jnp.where(kpos <= subs[u][4], st, NEG)
            m = m_ref[u]
            mn = jnp.maximum(m, jnp.max(st, axis=0, keepdims=True))
            ps.append((nk, jnp.exp(st - mn).astype(BF16), jnp.exp(m - mn)))
            m_ref[u] = mn
        for u, (nk, p, alpha) in enumerate(ps):
            acc_ref[u] = alpha * acc_ref[u] + _dot(vselt_ref[:, pl.ds(k0, nk)], p)

    jd = s00 // tk
    nw = min(WINDOW + ts, seq)
    w0s = [pl.multiple_of(jnp.maximum(sub[1] + ts - nw, 0), ts) for sub in subs]
    for u in range(nsub):
        sw_ref[u] = _dot_nt(kwin_ref[pl.ds(w0s[u], nw), :], subs[u][2])
    scores(sa_ref, 0)
    for u, (rs, s0, qrows, _, qpos, gtt) in enumerate(subs):
        d = qpos - (w0s[u] + lax.broadcasted_iota(jnp.int32, (nw, 1), 0))
        sw = jnp.where((d >= 0) & (d < WINDOW), sw_ref[u], NEG)
        pw = jnp.exp(sw - jnp.max(sw, axis=0, keepdims=True)).astype(BF16)
        accw = _dot(vwint_ref[:, pl.ds(w0s[u], nw)], pw)
        ow_ref[u] = _gate_lanes(gtt, 2, ts) * (accw[:HEAD_DIM] / accw[HEAD_DIM:HEAD_DIM + 1])

    def body(i, carry):
        j = 2 * i
        scores(sb_ref, j + 1)
        update(sa_ref, j, False)
        scores(sa_ref, j + 2)
        update(sb_ref, j + 1, False)
        return carry

    lax.fori_loop(0, jd // 2, body, 0)
    odd = lax.rem(jd, 2) == 1

    @pl.when(odd)
    def _():
        scores(sb_ref, jd)
        update(sa_ref, jd - 1, False)
        update(sb_ref, jd, True)

    @pl.when(jnp.logical_not(odd))
    def _():
        update(sa_ref, jd, True)

    for u, (rs, s0, qrows, _, qpos, gtt) in enumerate(subs):
        acc = acc_ref[u]
        o_t = ow_ref[u] + _gate_lanes(gtt, 1, ts) * (acc[:HEAD_DIM] / acc[HEAD_DIM:HEAD_DIM + 1])
        o_ref[rs, :] = (oc_ref[rs, :].astype(F32) + _ungroup_t(o_t, ts)).astype(BF16)


def _sel_win(q, nsel, gates, ocg, ksel, vselt, kwin, vwint, nb, seq):
    ts = 128
    tk = min(512, seq)
    nsub = tk // ts
    tq = ts * nsub
    nq = seq // tq
    gq = GROUP_SIZE * HEAD_DIM
    qspec = pl.BlockSpec((tq, gq), lambda b, g, i: (b * nq + i, g))
    krows = lambda w: pl.BlockSpec((None, seq, w), lambda b, g, i: (g, b, 0))
    vcols = pl.BlockSpec((None, 2 * HEAD_DIM, seq), lambda b, g, i: (g, 0, b))
    return pl.pallas_call(
        functools.partial(_sel_win_kernel, ts=ts, nsub=nsub, tk=tk, seq=seq),
        grid=(nb, KV_GROUPS, nq),
        in_specs=[
            qspec,
            pl.BlockSpec((None, None, tq, LANES), lambda b, g, i: (b, g, i, 0)),
            pl.BlockSpec((None, tq, LANES), lambda b, g, i: (g, b * nq + i, 0)),
            qspec,
            krows(LANES + HEAD_DIM), vcols, krows(HEAD_DIM), vcols,
        ],
        out_specs=qspec,
        out_shape=jax.ShapeDtypeStruct((nb * seq, Q_DIM), BF16),
        scratch_shapes=[pltpu.VMEM((nsub, tk, GROUP_SIZE * ts), F32),
                        pltpu.VMEM((nsub, tk, GROUP_SIZE * ts), F32),
                        pltpu.VMEM((nsub, min(WINDOW + ts, seq), GROUP_SIZE * ts), F32),
                        pltpu.VMEM((nsub, 1, GROUP_SIZE * ts), F32),
                        pltpu.VMEM((nsub, 2 * HEAD_DIM, GROUP_SIZE * ts), F32),
                        pltpu.VMEM((nsub, HEAD_DIM, GROUP_SIZE * ts), F32)],
        compiler_params=_params(("parallel", "parallel", "arbitrary")),
        name="sel_win_attn",
    )(q, nsel, gates, ocg, ksel, vselt, kwin, vwint)


def _sample_attn_kernel(pt_ref, pool_ref, qbd_ref, nselr_ref, eneg_ref, knew_ref, wnew_ref,
                        wbuf_ref, gate_ref, ocg_ref, o_ref,
                        kbuf, sem, m_sc, l_sc, acc_sc, *, pgs, nch, ppt, past, tn):
    b = pl.program_id(0)
    c = pl.program_id(1)
    step = b * nch + c
    nsteps = pl.num_programs(0) * nch

    def copies(st, slot):
        bb = st // nch
        cc = st % nch
        return [pltpu.make_async_copy(pool_ref.at[pt_ref[bb, cc * pgs + p]], kbuf.at[slot, p],
                                      sem.at[slot]) for p in range(pgs)]

    @pl.when(step == 0)
    def _():
        for cp in copies(step, 0):
            cp.start()

    @pl.when(step + 1 < nsteps)
    def _():
        for cp in copies(step + 1, (step + 1) % 2):
            cp.start()

    @pl.when(c == 0)
    def _():
        m_sc[...] = jnp.full(m_sc.shape, NEG, F32)
        l_sc[...] = jnp.zeros(l_sc.shape, F32)
        acc_sc[...] = jnp.zeros(acc_sc.shape, F32)

    slot = step % 2
    for cp in copies(step, slot):
        cp.wait()

    qbd = qbd_ref[...]
    nselr = nselr_ref[...]
    rows = qbd.shape[0]
    qpos = past + lax.rem(lax.broadcasted_iota(jnp.int32, (rows, 1), 0), tn)

    def online(s, pv):
        m = m_sc[...]
        mn = jnp.maximum(m, jnp.max(s, axis=-1, keepdims=True))
        p = jnp.exp(s - mn)
        alpha = jnp.exp(m - mn)
        l_sc[...] = alpha * l_sc[...] + jnp.sum(p, axis=-1, keepdims=True)
        acc_sc[...] = alpha * acc_sc[...] + pv(p.astype(BF16))
        m_sc[...] = mn

    tk = ppt * PAGE_SIZE
    for t in range(pgs // ppt):
        kt = jnp.concatenate([kbuf[slot, t * ppt + i, 0] for i in range(ppt)], axis=1).astype(BF16)
        vt = jnp.concatenate([kbuf[slot, t * ppt + i, 1] for i in range(ppt)], axis=1).astype(BF16)
        k0 = pl.multiple_of(c * (pgs * PAGE_SIZE) + t * tk, tk)
        bias = _dot_nt(nselr, eneg_ref[pl.ds(k0, tk), :])
        online(_dot(qbd, kt) + bias, lambda p, vt=vt: _dot_nt(p, vt))

    @pl.when(c == nch - 1)
    def _():
        npad = knew_ref.shape[0]
        newpos = past + lax.broadcasted_iota(jnp.int32, (1, npad), 1)
        new_ok = (newpos <= qpos) & (newpos < past + tn)
        kn = knew_ref[:, 0:GKV].astype(BF16)
        vn = knew_ref[:, GKV:].astype(BF16)
        online(jnp.where(new_ok, _dot_nt(qbd, kn), NEG), lambda p: _dot(p, vn))
        o_s = acc_sc[...] / l_sc[...]
        wbl = wbuf_ref.shape[2]
        dw = qpos - (past - wbl + lax.broadcasted_iota(jnp.int32, (1, wbl), 1))
        s1 = jnp.where((dw >= 0) & (dw < WINDOW), _dot(qbd, wbuf_ref[0].astype(BF16)), NEG)
        s2 = jnp.where(new_ok & (qpos - newpos < WINDOW),
                       _dot_nt(qbd, wnew_ref[:, 0:GKV].astype(BF16)), NEG)
        mw = jnp.maximum(jnp.max(s1, axis=-1, keepdims=True), jnp.max(s2, axis=-1, keepdims=True))
        p1 = jnp.exp(s1 - mw)
        p2 = jnp.exp(s2 - mw)
        lw = jnp.sum(p1, axis=-1, keepdims=True) + jnp.sum(p2, axis=-1, keepdims=True)
        o_w = (_dot_nt(p1.astype(BF16), wbuf_ref[1].astype(BF16))
               + _dot(p2.astype(BF16), wnew_ref[:, GKV:].astype(BF16))) / lw
        gt = gate_ref[...]
        o_ref[...] = ocg_ref[...] + gt[:, 1:2] * o_s + gt[:, 2:3] * o_w


def _sample_attn(pt, pool_t, qbd, nselr, eneg, knew, wnew, wbuf_t, wbuf_off, gates_r, ocg_r,
                 pgs, past, tn):
    nb, n_pages = pt.shape
    nch = n_pages // pgs
    rows = qbd.shape[1]
    ppt = pgs
    per_b = lambda shape: pl.BlockSpec((None,) + shape, lambda b, c, pt: (b,) + (0,) * len(shape))
    grid_spec = pltpu.PrefetchScalarGridSpec(
        num_scalar_prefetch=1,
        grid=(nb, nch),
        in_specs=[
            pl.BlockSpec(memory_space=pl.ANY),
            per_b(qbd.shape[1:]),
            per_b(nselr.shape[1:]),
            pl.BlockSpec(eneg.shape, lambda b, c, pt: (0, 0), pipeline_mode=pl.Buffered(1)),
            per_b(knew.shape[1:]),
            per_b(wnew.shape[1:]),
            pl.BlockSpec((None,) + wbuf_t.shape[1:], lambda b, c, pt: (wbuf_off + b, 0, 0, 0)),
            per_b(gates_r.shape[1:]),
            per_b(ocg_r.shape[1:]),
        ],
        out_specs=per_b((rows, GKV)),
        scratch_shapes=[
            pltpu.VMEM((2, pgs, 2, GKV, PAGE_SIZE), F32),
            pltpu.SemaphoreType.DMA((2,)),
            pltpu.VMEM((rows, 1), F32),
            pltpu.VMEM((rows, 1), F32),
            pltpu.VMEM((rows, GKV), F32),
        ],
    )
    return pl.pallas_call(
        functools.partial(_sample_attn_kernel, pgs=pgs, nch=nch, ppt=ppt, past=past, tn=tn),
        grid_spec=grid_spec,
        out_shape=jax.ShapeDtypeStruct((nb, rows, GKV), F32),
        compiler_params=_params(("arbitrary", "arbitrary")),
        name="sample_attn",
    )(pt, pool_t, qbd, nselr, eneg, knew, wnew, wbuf_t, gates_r, ocg_r)


def _overlap_t(nbp, nrow):
    cs = np.arange(nrow)[None, :] * CMP_STRIDE
    ss = np.arange(nbp)[:, None] * SEL_BLOCK
    ov = np.minimum(cs + CMP_BLOCK, ss + SEL_BLOCK) - np.maximum(cs, ss)
    return jnp.asarray(np.clip(ov, 0, None).astype(np.float32) / CMP_BLOCK, dtype=BF16)


def _block_diag_w1(w1):
    eye = jnp.eye(KV_GROUPS, dtype=w1.dtype)
    bd = jnp.einsum('gh,klde->klgdhe', eye, w1).reshape(2, CMP_BLOCK, GKV, GKV)
    return jnp.concatenate([bd[:, :CMP_STRIDE], bd[:, CMP_STRIDE:]], axis=-1).astype(BF16)


def _block_diag_w2(w2):
    eye = jnp.eye(KV_GROUPS, dtype=w2.dtype)
    return jnp.einsum('gh,kde->kgdhe', eye, w2).reshape(2, GKV, GKV).astype(BF16)


def _gate_weight(wg):
    d = wg.shape[0]
    w = wg.reshape(d, KV_GROUPS, 3 * GROUP_SIZE)
    w = jnp.pad(w, ((0, 0), (0, 0), (0, LANES - 3 * GROUP_SIZE)))
    return w.reshape(d, KV_GROUPS * LANES).astype(BF16)


def _token_minor(cache):
    nd = cache.ndim
    perm = tuple(range(nd - 4)) + (nd - 3, nd - 2, nd - 1, nd - 4)
    t = cache.transpose(perm)
    return t.reshape(t.shape[:nd - 3] + (GKV, t.shape[-1]))


def _from_pages(pages_t, lead):
    x = pages_t.reshape(lead + (2, KV_GROUPS, HEAD_DIM, PAGE_SIZE))
    nd = x.ndim
    return x.transpose(tuple(range(nd - 4)) + (nd - 1, nd - 4, nd - 3, nd - 2))


def _nsa_layer(yp, ys, a, cache_cmp_kv, cache_sel_kv, cache_win_kv, page_table, gains,
               w_nsa_in, w_cmp_hidden, w_cmp_out, cmp_pos_emb, dims):
    bsz, seq, db, tn, past = dims
    n_pool = cache_cmp_kv.shape[1]
    n_pages = page_table.shape[1]
    w_in = w_nsa_in[a]
    wq = w_in[:, :Q_DIM].astype(BF16)
    wkv = w_in[:, Q_DIM:Q_DIM + 3 * KV_DIM].astype(BF16)
    wg = _gate_weight(w_in[:, Q_DIM + 3 * KV_DIM:])
    bd1 = _block_diag_w1(w_cmp_hidden[a])
    w2bd = _block_diag_w2(w_cmp_out[a])
    w1f = w_cmp_hidden[a].reshape(2, CMP_BLOCK * HEAD_DIM, HEAD_DIM).astype(BF16)
    pe8 = jnp.broadcast_to(cmp_pos_emb[a].reshape(2, 1, CMP_BLOCK * HEAD_DIM),
                           (2, 8, CMP_BLOCK * HEAD_DIM)).astype(BF16)

    tm = min(512, seq)
    (qp, kvc_p, _, _, kvct_p, kvst_p, kvwt_p, gate_p, ksel, vselt, kwin, vwint) = _nsa_proj(
        yp, gains, wq, wkv, wg, tm, seq)
    nrow = seq // CMP_STRIDE
    kc, vct = _compress_fin(_compress_ab(kvc_p, bd1, bsz), pe8, w1f, w2bd)
    ns_p = -(-seq // SEL_BLOCK)
    assert ns_p <= LANES
    ocg, nsel = _cmp_select(qp, kc, vct, _overlap_t(LANES, nrow), gate_p, seq, 0,
                            min(SEL_TOPN, ns_p), ns_p)
    op = _sel_win(qp, nsel, gate_p, ocg, ksel, vselt, kwin, vwint, bsz, seq)

    rows_s = db * tn
    qs, kvc_s, kvs_s, kvw_s, _, _, _, gate_s, _, _, _, _ = _nsa_proj(
        ys, gains, wq, wkv, wg, rows_s, rows_s)
    pt_abs = page_table + a * n_pool
    pool_c = _token_minor(cache_cmp_kv).reshape(-1, 2, GKV, PAGE_SIZE)
    pool_s = _token_minor(cache_sel_kv).reshape(-1, 2, GKV, PAGE_SIZE)
    ab_s = _compress_ab_paged(pt_abs, pool_c, bd1, min(32, n_pages))
    kc_s, vct_s = _compress_fin(ab_s, pe8, w1f, w2bd)
    nrow_s = past // CMP_STRIDE
    assert tn <= SEL_BLOCK and past // SEL_BLOCK <= LANES
    ns_s = past // SEL_BLOCK
    nbp_s = -(-ns_s // LANES) * LANES
    tpad = 128
    qs_pad = jnp.pad(qs.reshape(db, tn, Q_DIM), ((0, 0), (0, tpad - tn), (0, 0)))
    gate_pad = jnp.pad(gate_s.reshape(KV_GROUPS, db, tn, LANES),
                       ((0, 0), (0, 0), (0, tpad - tn), (0, 0)))
    ocg_s, nsel_s = _cmp_select(qs_pad.reshape(db * tpad, Q_DIM), kc_s, vct_s,
                                _overlap_t(nbp_s, nrow_s),
                                gate_pad.reshape(KV_GROUPS, db * tpad, LANES), tpad, past,
                                min(SEL_TOPN, ns_s + 1) - 1, ns_s)
    eye = jnp.eye(KV_GROUPS, dtype=BF16)
    q5 = qs.reshape(db, tn, KV_GROUPS, GROUP_SIZE, HEAD_DIM)
    rows = KV_GROUPS * GROUP_SIZE * tn
    qbd = jnp.einsum('bqgrd,gh->bgrqhd', q5, eye).reshape(db, rows, GKV)
    nselr = jnp.broadcast_to(nsel_s[:, :, None, :tn, :LANES],
                             (db, KV_GROUPS, GROUP_SIZE, tn, LANES)).reshape(db, rows, LANES)
    g4 = gate_s.reshape(KV_GROUPS, db, tn, LANES)[..., :3 * GROUP_SIZE]
    g4 = g4.reshape(KV_GROUPS, db, tn, GROUP_SIZE, 3).transpose(1, 0, 3, 2, 4)
    gates_r = jnp.pad(g4.reshape(db, rows, 3), ((0, 0), (0, 0), (0, LANES - 3)))
    oc5 = ocg_s.reshape(db, tpad, KV_GROUPS, GROUP_SIZE, HEAD_DIM)[:, :tn].astype(F32)
    ocg_r = jnp.einsum('bqgrd,gh->bgrqhd', oc5, jnp.eye(KV_GROUPS, dtype=F32)).reshape(
        db, rows, GKV)
    kpos = np.arange(past)[:, None] // SEL_BLOCK
    eneg = jnp.asarray(np.where(kpos == np.arange(LANES)[None, :], -MASK_BIG, 0.0), dtype=BF16)
    npad = 128
    knew = jnp.pad(kvs_s.reshape(db, tn, KV_DIM), ((0, 0), (0, npad - tn), (0, 0)))
    wnew = jnp.pad(kvw_s.reshape(db, tn, KV_DIM), ((0, 0), (0, npad - tn), (0, 0)))
    wbl = cache_win_kv.shape[2]
    assert past >= wbl
    wbuf_t = _token_minor(cache_win_kv).reshape(-1, 2, GKV, wbl)
    o_rows = _sample_attn(pt_abs, pool_s, qbd, nselr, eneg, knew, wnew, wbuf_t, a * db,
                          gates_r, ocg_r, min(16, n_pages), past, tn)
    o6 = o_rows.reshape(db, KV_GROUPS, GROUP_SIZE, tn, KV_GROUPS, HEAD_DIM)
    o_s = jnp.einsum('bgrqhd,gh->bqgrd', o6, jnp.eye(KV_GROUPS, dtype=F32))
    o_s = o_s.reshape(rows_s, Q_DIM).astype(BF16)

    shp = (KV_GROUPS, HEAD_DIM)
    npg = seq // PAGE_SIZE
    wpg = min(WINDOW, seq) // PAGE_SIZE
    p_win = _from_pages(kvwt_p.reshape(bsz, npg, KV_DIM, PAGE_SIZE)[:, npg - wpg:], (bsz, wpg))
    kvw_s5 = kvw_s.reshape((db, tn, 2) + shp)
    caches = (_from_pages(kvct_p, (bsz, npg)), _from_pages(kvst_p, (bsz, npg)),
              p_win.reshape((bsz, wpg * PAGE_SIZE, 2) + shp),
              kvc_s.reshape((db, tn, 2) + shp), kvs_s.reshape((db, tn, 2) + shp), kvw_s5)
    return op, o_s, caches


def kernel(x_prompt, x_sample, cache_cmp_kv, cache_sel_kv, cache_win_kv, page_table, norm_gains,
           w_nsa_in, w_cmp_hidden, w_cmp_out, cmp_pos_emb, w_nsa_out, w_gm_in, gm_norm_gain,
           w_spatial, b_spatial, w_gm_out, w_ffn_in, w_ffn_out):
    bsz, seq, d = x_prompt.shape
    db, tn, _ = x_sample.shape
    past = page_table.shape[1] * PAGE_SIZE
    depth = norm_gains.shape[0]
    assert seq % CHUNK == 0 and past % SEL_BLOCK == 0 and tn < CMP_STRIDE
    assert (db * tn) % 8 == 0 and CHUNK % tn == 0 and db * tn == CHUNK
    dims = (bsz, seq, db, tn, past)
    yp = x_prompt.reshape(bsz * seq, d)
    ys = x_sample.reshape(db * tn, d)
    tm_p = min(512, seq)
    tm_s = db * tn
    lists = [[] for _ in range(7)]
    w_fin = w_ffn_in.astype(BF16)
    w_fout = w_ffn_out.astype(BF16)
    for i in range(depth):
        gains = norm_gains[i]
        if i % 2 == 0:
            a = i // 2
            mp, ms, caches = _nsa_layer(yp, ys, a, cache_cmp_kv, cache_sel_kv, cache_win_kv,
                                        page_table, gains, w_nsa_in, w_cmp_hidden, w_cmp_out,
                                        cmp_pos_emb, dims)
            for lst, c in zip(lists[:6], caches):
                lst.append(c)
            wo = w_nsa_out[a].astype(BF16)
        else:
            bi = i // 2
            w_in = w_gm_in[bi].astype(BF16)
            ln_g = gm_norm_gain[bi].reshape(1, -1)
            ws = w_spatial[bi]
            bs = b_spatial[bi]
            mp = _gmlp(yp, gains, w_in, ln_g, ws, bs.T, tm_p, False)[0]
            eye = jnp.eye(db, dtype=ws.dtype)
            ws_s = jnp.einsum('bc,gts->gbtcs', eye, ws[:, :tn, :tn]).reshape(-1, tm_s, tm_s)
            bs_s = jnp.tile(bs[:, :tn], (1, db)).T
            ms, v_new = _gmlp(ys, gains, w_in, ln_g, ws_s, bs_s, tm_s, True)
            lists[6].append(v_new.reshape(db, tn, -1))
            wo = w_gm_out[bi].astype(BF16)
        yp = _post(mp, yp, wo, gains, w_fin, w_fout, i, tm_p)
        ys = _post(ms, ys, wo, gains, w_fin, w_fout, i, tm_s)
    outs = [jnp.stack(l) for l in lists]
    assert cache_win_kv.shape[2] >= tn
    outs[5] = jnp.concatenate([cache_win_kv[:, :, tn:], outs[5]], axis=2)
    return (yp.reshape(bsz, seq, d), ys.reshape(db, tn, d)) + tuple(outs)
```
